```python
import math, functools
import jax, jax.numpy as jnp
from jax import lax
import numpy as np

D_MODEL = 4096
BATCH = 2
SEQ = 8192
DEPTH = 1

MEM_LEN = 256
MEM_HEADS = 4
MEM_HEAD_DIM = 256
LRU_WIDTH = D_MODEL
LRU_HEADS = 16
LRU_BLOCK_W = LRU_WIDTH // LRU_HEADS
CONV_WIDTH = 4
LRU_C = 8.0
MLA_HEADS = 32
QK_NOPE = 128
QK_ROPE = 64
V_HEAD = 128
Q_LORA = 1024
KV_LORA = 512
ROPE_THETA = 10000.0
Q_BLOCK = 128
D_FF = -(-8 * D_MODEL // (3 * 256)) * 256
EPS = 1e-6

IN_SPLITS = (LRU_WIDTH, LRU_WIDTH, Q_LORA, KV_LORA, QK_ROPE, D_MODEL, D_MODEL)
IN_COLS = sum(IN_SPLITS)

kernel_name = "hybrid_rglru_mla_memory_encoder"


def _rmsnorm(t, g):
    t32 = t.astype(jnp.float32)
    inv = lax.rsqrt(jnp.mean(t32 * t32, axis=-1, keepdims=True) + EPS)
    return (t32 * inv * g.astype(jnp.float32)).astype(t.dtype)


def _split_cols(t, sizes):
    idx = np.cumsum(sizes)[:-1].tolist()
    return jnp.split(t, idx, axis=-1)


def _rope_tables(positions):
    inv_freq = ROPE_THETA ** (-jnp.arange(0, QK_ROPE, 2, dtype=jnp.float32) / QK_ROPE)
    ang = positions.astype(jnp.float32)[..., None] * inv_freq
    return jnp.cos(ang)[:, :, None, :], jnp.sin(ang)[:, :, None, :]


def _apply_rope(t, cos, sin):
    t32 = t.astype(jnp.float32)
    t1, t2 = jnp.split(t32, 2, axis=-1)
    return jnp.concatenate([t1 * cos - t2 * sin, t2 * cos + t1 * sin], axis=-1).astype(t.dtype)


def _depthwise_conv_centred(t, w, b):
    left = CONV_WIDTH // 2
    right = CONV_WIDTH - 1 - left
    s = t.shape[1]
    tp = jnp.pad(t, ((0, 0), (left, right), (0, 0)))
    out = b.astype(t.dtype)
    for k in range(CONV_WIDTH):
        out = out + tp[:, k:k + s] * w[k]
    return out


def _scan_combine(c1, c2):
    a1, b1 = c1
    a2, b2 = c2
    return a1 * a2, a2 * b1 + b2


def _rg_lru(xc, wa, ba, wx, bx, lam, reverse):
    bsz, s, w = xc.shape
    xb = xc.reshape(bsz, s, LRU_HEADS, LRU_BLOCK_W)
    r = jax.nn.sigmoid(jnp.einsum('bsnd,nde->bsne', xb, wa) + ba).reshape(bsz, s, w)
    i = jax.nn.sigmoid(jnp.einsum('bsnd,nde->bsne', xb, wx) + bx).reshape(bsz, s, w)
    log_a = LRU_C * r.astype(jnp.float32) * jax.nn.log_sigmoid(lam.astype(jnp.float32))
    a = jnp.exp(log_a)
    mult = jnp.sqrt(-jnp.expm1(2.0 * log_a))
    pos = jnp.arange(s)
    first = (pos == (s - 1)) if reverse else (pos == 0)
    mult = jnp.where(first[None, :, None], 1.0, mult)
    bterm = mult * (i * xc).astype(jnp.float32)
    _, h = lax.associative_scan(_scan_combine, (a, bterm), reverse=reverse, axis=1)
    return h.astype(xc.dtype)


def _mla_attention(q, k, v):
    bsz, s, h, dqk = q.shape
    nblk = s // Q_BLOCK
    q = q * jnp.asarray((QK_NOPE + QK_ROPE) ** -0.5, q.dtype)
    qb = q.reshape(bsz, nblk, Q_BLOCK, h, dqk).transpose(1, 0, 2, 3, 4)

    def one_block(q_blk):
        sc = jnp.einsum('bqhd,bkhd->bhqk', q_blk, k, preferred_element_type=jnp.float32)
        p = jax.nn.softmax(sc, axis=-1)
        return jnp.einsum('bhqk,bkhd->bqhd', p.astype(v.dtype), v)

    o = lax.map(one_block, qb)
    return o.transpose(1, 0, 2, 3, 4).reshape(bsz, s, h * V_HEAD)


def setup_inputs(seed: int = 0) -> dict:
    key = jax.random.key(seed)
    ks = iter(jax.random.split(key, 48))
    L = DEPTH

    def nrm(shape, scale):
        return jax.random.normal(next(ks), shape, jnp.float32) * scale

    def gain(shape):
        return 1.0 + 0.01 * jax.random.normal(next(ks), shape, jnp.float32)

    def lru_lambda():
        a8 = jax.random.uniform(next(ks), (L, LRU_WIDTH), jnp.float32, 0.9, 0.999)
        a = a8 ** (1.0 / LRU_C)
        return jnp.log(a) - jnp.log1p(-a)

    x = jax.random.normal(next(ks), (BATCH, SEQ, D_MODEL), jnp.float32)
    mem = jax.random.normal(next(ks), (BATCH, MEM_LEN, D_MODEL), jnp.float32)
    offset = jax.random.randint(next(ks), (BATCH, 1), 0, 4096, dtype=jnp.int32)
    positions = offset + jnp.arange(SEQ, dtype=jnp.int32)[None, :]
    bw = LRU_BLOCK_W ** -0.5
    return {
        "x": x,
        "mem": mem,
        "positions": positions,
        "norm_mix": gain((L, D_MODEL)),
        "w_in": nrm((L, D_MODEL, IN_COLS), D_MODEL ** -0.5),
        "conv_w": nrm((L, CONV_WIDTH, LRU_WIDTH), CONV_WIDTH ** -0.5),
        "conv_b": nrm((L, LRU_WIDTH), 0.01),
        "lru_wa_f": nrm((L, LRU_HEADS, LRU_BLOCK_W, LRU_BLOCK_W), bw),
        "lru_ba_f": nrm((L, LRU_HEADS, LRU_BLOCK_W), 0.01),
        "lru_wx_f": nrm((L, LRU_HEADS, LRU_BLOCK_W, LRU_BLOCK_W), bw),
        "lru_bx_f": nrm((L, LRU_HEADS, LRU_BLOCK_W), 0.01),
        "lru_lam_f": lru_lambda(),
        "lru_wa_b": nrm((L, LRU_HEADS, LRU_BLOCK_W, LRU_BLOCK_W), bw),
        "lru_ba_b": nrm((L, LRU_HEADS, LRU_BLOCK_W), 0.01),
        "lru_wx_b": nrm((L, LRU_HEADS, LRU_BLOCK_W, LRU_BLOCK_W), bw),
        "lru_bx_b": nrm((L, LRU_HEADS, LRU_BLOCK_W), 0.01),
        "lru_lam_b": lru_lambda(),
        "q_a_norm": gain((L, Q_LORA)),
        "w_uq": nrm((L, Q_LORA, MLA_HEADS * (QK_NOPE + QK_ROPE)), Q_LORA ** -0.5),
        "kv_a_norm": gain((L, KV_LORA)),
        "w_ukv": nrm((L, KV_LORA, MLA_HEADS * (QK_NOPE + V_HEAD)), KV_LORA ** -0.5),
        "w_proj_lru": nrm((L, LRU_WIDTH, D_MODEL), LRU_WIDTH ** -0.5),
        "w_proj_mla": nrm((L, MLA_HEADS * V_HEAD, D_MODEL), (MLA_HEADS * V_HEAD) ** -0.5),
        "w_out": nrm((L, D_MODEL, D_MODEL), D_MODEL ** -0.5),
        "norm_mem_x": gain((L, D_MODEL)),
        "norm_mem_kv": gain((L, D_MODEL)),
        "w_mem_q": nrm((L, D_MODEL, MEM_HEADS * MEM_HEAD_DIM), D_MODEL ** -0.5),
        "w_mem_kv": nrm((L, D_MODEL, 2 * MEM_HEADS * MEM_HEAD_DIM), D_MODEL ** -0.5),
        "w_mem_o": nrm((L, MEM_HEADS * MEM_HEAD_DIM, D_MODEL), (MEM_HEADS * MEM_HEAD_DIM) ** -0.5),
        "norm_ffn": gain((L, D_MODEL)),
        "w_ffn_gate": nrm((L, D_MODEL, D_FF), D_MODEL ** -0.5),
        "w_ffn_up": nrm((L, D_MODEL, D_FF), D_MODEL ** -0.5),
        "w_ffn_down": nrm((L, D_FF, D_MODEL), D_FF ** -0.5),
        "norm_final": gain((D_MODEL,)),
    }


def reference(x, mem, positions, norm_mix, w_in, conv_w, conv_b,
              lru_wa_f, lru_ba_f, lru_wx_f, lru_bx_f, lru_lam_f,
              lru_wa_b, lru_ba_b, lru_wx_b, lru_bx_b, lru_lam_b,
              q_a_norm, w_uq, kv_a_norm, w_ukv,
              w_proj_lru, w_proj_mla, w_out,
              norm_mem_x, norm_mem_kv, w_mem_q, w_mem_kv, w_mem_o,
              norm_ffn, w_ffn_gate, w_ffn_up, w_ffn_down, norm_final):
    bsz, s, _ = x.shape
    cos, sin = _rope_tables(positions)

    for l in range(DEPTH):
        h = _rmsnorm(x, norm_mix[l])
        proj = h @ w_in[l]
        xa, ga, cq, ckv, krope, gate_a, gate_b = _split_cols(proj, IN_SPLITS)

        xc = _depthwise_conv_centred(xa, conv_w[l], conv_b[l])
        y_lru = (_rg_lru(xc, lru_wa_f[l], lru_ba_f[l], lru_wx_f[l], lru_bx_f[l], lru_lam_f[l], False)
                 + _rg_lru(xc, lru_wa_b[l], lru_ba_b[l], lru_wx_b[l], lru_bx_b[l], lru_lam_b[l], True))
        y_a = jax.nn.gelu(ga) * y_lru

        q = (_rmsnorm(cq, q_a_norm[l]) @ w_uq[l]).reshape(bsz, s, MLA_HEADS, QK_NOPE + QK_ROPE)
        q_nope, q_rope = jnp.split(q, [QK_NOPE], axis=-1)
        q_rope = _apply_rope(q_rope, cos, sin)
        kv = (_rmsnorm(ckv, kv_a_norm[l]) @ w_ukv[l]).reshape(bsz, s, MLA_HEADS, QK_NOPE + V_HEAD)
        k_nope, v = jnp.split(kv, [QK_NOPE], axis=-1)
        k_rope = _apply_rope(krope[:, :, None, :], cos, sin)
        k_rope = jnp.broadcast_to(k_rope, (bsz, s, MLA_HEADS, QK_ROPE))
        qh = jnp.concatenate([q_nope, q_rope], axis=-1)
        kh = jnp.concatenate([k_nope, k_rope], axis=-1)
        y_b = _mla_attention(qh, kh, v)

        merged = (jax.nn.sigmoid(gate_a) * (y_a @ w_proj_lru[l])
                  + jax.nn.sigmoid(gate_b) * (y_b @ w_proj_mla[l]))
        x = x + merged @ w_out[l]

        hq = _rmsnorm(x, norm_mem_x[l])
        hm = _rmsnorm(mem, norm_mem_kv[l])
        mq = (hq @ w_mem_q[l]).reshape(bsz, s, MEM_HEADS, MEM_HEAD_DIM)
        mk, mv = jnp.split((hm @ w_mem_kv[l]).reshape(bsz, -1, MEM_HEADS, 2 * MEM_HEAD_DIM), 2, axis=-1)
        sc = jnp.einsum('bqhd,bkhd->bhqk', mq * jnp.asarray(MEM_HEAD_DIM ** -0.5, mq.dtype), mk,
                        preferred_element_type=jnp.float32)
        p = jax.nn.softmax(sc, axis=-1).astype(mv.dtype)
        mo = jnp.einsum('bhqk,bkhd->bqhd', p, mv).reshape(bsz, s, MEM_HEADS * MEM_HEAD_DIM)
        x = x + mo @ w_mem_o[l]

        hf = _rmsnorm(x, norm_ffn[l])
        x = x + (jax.nn.silu(hf @ w_ffn_gate[l]) * (hf @ w_ffn_up[l])) @ w_ffn_down[l]

    return _rmsnorm(x, norm_final)
```

```python
import functools

import jax
import jax.numpy as jnp
from jax import lax
from jax.experimental import pallas as pl
from jax.experimental.pallas import tpu as pltpu

F32 = jnp.float32
BF16 = jnp.bfloat16

EPS = 1e-6
CONV_WIDTH = 4
LRU_C = 8.0
MLA_HEADS = 32
QK_NOPE = 128
QK_ROPE = 64
V_HEAD = 128
ROPE_THETA = 10000.0
MEM_HEADS = 4

LANES = 128
SUBLANES = 8
VMEM_LIMIT_BYTES = 56 * 1024 * 1024

QK_PAD = 2 * LANES
SCAN_PITCH_PAD = 4


def _tile(n, pref, align):
    if n <= pref:
        return n
    t = (pref // align) * align
    while t > align and n % t:
        t -= align
    assert n % t == 0, (n, pref, align)
    return t


def _params(*sem):
    return pltpu.CompilerParams(dimension_semantics=sem, vmem_limit_bytes=VMEM_LIMIT_BYTES)


def _rms(x, g):
    inv = lax.rsqrt(jnp.mean(x * x, axis=-1, keepdims=True) + EPS)
    return x * inv * g


def _rmsnorm_kernel(x_ref, g_ref, o_ref):
    o_ref[...] = _rms(x_ref[...].astype(F32), g_ref[...]).astype(o_ref.dtype)


def _rmsnorm(x, g, out_dtype, name):
    m, d = x.shape
    tm = _tile(m, 256, SUBLANES)
    return pl.pallas_call(
        _rmsnorm_kernel,
        out_shape=jax.ShapeDtypeStruct((m, d), out_dtype),
        grid=(m // tm,),
        in_specs=[pl.BlockSpec((tm, d), lambda i: (i, 0)), pl.BlockSpec((1, d), lambda i: (0, 0))],
        out_specs=pl.BlockSpec((tm, d), lambda i: (i, 0)),
        compiler_params=_params("parallel"),
        name=name,
    )(x, g.reshape(1, d).astype(F32))


def _mm_kernel(*refs, n_a, pairs, n_tile, n_row, n_col, epilogue):
    a_refs = refs[:n_a]
    w_refs = refs[n_a:n_a + len(pairs)]
    p = n_a + len(pairs)
    tile_refs = refs[p:p + n_tile]
    row_refs = refs[p + n_tile:p + n_tile + n_row]
    col_refs = refs[p + n_tile + n_row:p + n_tile + n_row + n_col]
    o_refs = refs[p + n_tile + n_row + n_col:]
    accs = [jnp.dot(a_refs[ai][...], w_ref[...], preferred_element_type=F32)
            for ai, w_ref in zip(pairs, w_refs)]
    epilogue(accs, tile_refs, row_refs, col_refs, o_refs)


def _mm(a_ops, w_ops, pairs, outs, epilogue, *, tm, tn, tile_ops=(), row_ops=(), col_ops=(), name):
    m = a_ops[0].shape[0]
    n = w_ops[0].shape[1]
    assert m % tm == 0 and n % tn == 0, (m, n, tm, tn)
    in_specs = [pl.BlockSpec((tm, a.shape[1]), lambda i, j: (i, 0)) for a in a_ops]
    in_specs += [pl.BlockSpec((w.shape[0], tn), lambda i, j: (0, j)) for w in w_ops]
    in_specs += [pl.BlockSpec((tm, tn), functools.partial(lambda off, i, j: (i, j + off), off))
                 for _, off in tile_ops]
    in_specs += [pl.BlockSpec((tm, r.shape[1]), lambda i, j: (i, 0)) for r in row_ops]
    in_specs += [pl.BlockSpec((1, tn), lambda i, j: (0, j)) for _ in col_ops]
    out_shape = [jax.ShapeDtypeStruct((m, (n // tn) * w), dt) for w, dt in outs]
    out_specs = [pl.BlockSpec((tm, w), lambda i, j: (i, j)) for w, _ in outs]
    kern = functools.partial(_mm_kernel, n_a=len(a_ops), pairs=tuple(pairs), n_tile=len(tile_ops),
                             n_row=len(row_ops), n_col=len(col_ops), epilogue=epilogue)
    res = pl.pallas_call(
        kern,
        out_shape=out_shape,
        grid=(m // tm, n // tn),
        in_specs=in_specs,
        out_specs=out_specs,
        compiler_params=_params("parallel", "arbitrary"),
        name=name,
    )(*a_ops, *w_ops, *[t for t, _ in tile_ops], *row_ops, *col_ops)
    return res


def _ep_plain(accs, tiles, rows, cols, outs):
    outs[0][...] = accs[0].astype(outs[0].dtype)


def _ep_gelu(accs, tiles, rows, cols, outs):
    outs[0][...] = jax.nn.gelu(accs[0]).astype(outs[0].dtype)


def _ep_sigmoid(accs, tiles, rows, cols, outs):
    outs[0][...] = jax.nn.sigmoid(accs[0]).astype(outs[0].dtype)


def _ep_rms(accs, tiles, rows, cols, outs):
    outs[0][...] = _rms(accs[0], cols[0][...]).astype(outs[0].dtype)


def _ep_scale(scale, accs, tiles, rows, cols, outs):
    outs[0][...] = (accs[0] * scale).astype(outs[0].dtype)


def _ep_residual(accs, tiles, rows, cols, outs):
    outs[0][...] = (tiles[0][...] + accs[0]).astype(outs[0].dtype)


def _ep_merge(accs, tiles, rows, cols, outs):
    outs[0][...] = (tiles[0][...] * accs[0] + tiles[1][...] * accs[1]).astype(outs[0].dtype)


def _ep_swiglu(accs, tiles, rows, cols, outs):
    outs[0][...] = (jax.nn.silu(accs[0]) * accs[1]).astype(outs[0].dtype)


def _rope(x, cos, sin_lo, sin_hi):
    half = QK_ROPE // 2
    return x * cos + pltpu.roll(x, half, 1) * sin_hi + pltpu.roll(x, LANES - half, 1) * sin_lo


def _ep_ckv(kv_lora, accs, tiles, rows, cols, outs):
    acc = accs[0]
    outs[0][...] = _rms(acc[:, :kv_lora], cols[0][:, :kv_lora]).astype(outs[0].dtype)
    kr = _rope(acc[:, kv_lora:kv_lora + LANES], rows[0][...], rows[1][...], rows[2][...])
    outs[1][...] = kr.astype(outs[1].dtype)


def _ep_q(scale, accs, tiles, rows, cols, outs):
    acc = accs[0]
    cos, sin_lo, sin_hi = rows[0][...], rows[1][...], rows[2][...]
    for h in range(acc.shape[1] // QK_PAD):
        c0 = h * QK_PAD
        outs[0][:, c0:c0 + QK_NOPE] = (acc[:, c0:c0 + QK_NOPE] * scale).astype(outs[0].dtype)
        rp = _rope(acc[:, c0 + QK_NOPE:c0 + QK_PAD], cos, sin_lo, sin_hi) * scale
        outs[0][:, c0 + QK_NOPE:c0 + QK_PAD] = rp.astype(outs[0].dtype)


def _ep_k(accs, tiles, rows, cols, outs):
    acc = accs[0]
    kr = rows[0][...]
    for h in range(acc.shape[1] // QK_NOPE):
        outs[0][:, h * QK_PAD:h * QK_PAD + QK_NOPE] = acc[:, h * QK_NOPE:(h + 1) * QK_NOPE].astype(outs[0].dtype)
        outs[0][:, h * QK_PAD + QK_NOPE:(h + 1) * QK_PAD] = kr


def _mm_kgrid_kernel(a_ref, w_ref, r_ref, o_ref, acc_ref):
    k = pl.program_id(2)

    @pl.when(k == 0)
    def _():
        acc_ref[...] = jnp.zeros_like(acc_ref)

    acc_ref[...] += jnp.dot(a_ref[...], w_ref[...], preferred_element_type=F32)

    @pl.when(k == pl.num_programs(2) - 1)
    def _():
        o_ref[...] = (r_ref[...] + acc_ref[...]).astype(o_ref.dtype)


def _mm_kgrid_residual(a, w, res, *, tm, tn, tk, name):
    m, kdim = a.shape
    n = w.shape[1]
    assert m % tm == 0 and n % tn == 0 and kdim % tk == 0
    return pl.pallas_call(
        _mm_kgrid_kernel,
        out_shape=jax.ShapeDtypeStruct((m, n), res.dtype),
        grid=(m // tm, n // tn, kdim // tk),
        in_specs=[pl.BlockSpec((tm, tk), lambda i, j, k: (i, k)),
                  pl.BlockSpec((tk, tn), lambda i, j, k: (k, j)),
                  pl.BlockSpec((tm, tn), lambda i, j, k: (i, j))],
        out_specs=pl.BlockSpec((tm, tn), lambda i, j, k: (i, j)),
        scratch_shapes=[pltpu.VMEM((tm, tn), F32)],
        compiler_params=_params("parallel", "parallel", "arbitrary"),
        name=name,
    )(a, w, res)


def _rope_table_kernel(pos_ref, f_ref, cos_ref, slo_ref, shi_ref):
    ang = pos_ref[...] * f_ref[...]
    lane = lax.broadcasted_iota(jnp.int32, ang.shape, 1)
    half = QK_ROPE // 2
    c = jnp.cos(ang)
    s = jnp.sin(ang)
    cos_ref[...] = jnp.where(lane < QK_ROPE, c, 0.0)
    slo_ref[...] = jnp.where(lane < half, -s, 0.0)
    shi_ref[...] = jnp.where((lane >= half) & (lane < QK_ROPE), s, 0.0)


def _rope_tables(positions):
    t = positions.size
    inv_freq = ROPE_THETA ** (-jnp.arange(0, QK_ROPE, 2, dtype=F32) / QK_ROPE)
    f = jnp.concatenate([inv_freq, inv_freq, jnp.zeros((LANES - QK_ROPE,), F32)]).reshape(1, LANES)
    pos = positions.astype(F32).reshape(t, 1)
    tm = _tile(t, 1024, SUBLANES)
    shp = jax.ShapeDtypeStruct((t, LANES), F32)
    return pl.pallas_call(
        _rope_table_kernel,
        out_shape=[shp, shp, shp],
        grid=(t // tm,),
        in_specs=[pl.BlockSpec((tm, 1), lambda i: (i, 0)), pl.BlockSpec((1, LANES), lambda i: (0, 0))],
        out_specs=[pl.BlockSpec((tm, LANES), lambda i: (i, 0))] * 3,
        compiler_params=_params("parallel"),
        name="rope_tables",
    )(pos, f)


def _log_sigmoid(x):
    return jnp.minimum(x, 0.0) - jnp.log1p(jnp.exp(-jnp.abs(x)))


def _lru_kernel(*refs, reverse, combine, ts, nc, seq):
    (xa_ref, prev_ref, next_ref, cw_ref, cb_ref, wa_ref, ba_ref, wx_ref, bx_ref, lam_ref) = refs[:10]
    if combine:
        hf_ref, gg_ref = refs[10:12]
        refs = refs[12:]
    else:
        refs = refs[10:]
    o_ref, buf, a_s, b_s, h_s, p_s, carry = refs
    bw = xa_ref.shape[-1]
    nl = bw // LANES
    seg = ts // SUBLANES
    pitch = seg + SCAN_PITCH_PAD
    halo = SUBLANES

    c = pl.program_id(2)
    cc = (nc - 1 - c) if reverse else c

    @pl.when(c == 0)
    def _():
        carry[...] = jnp.zeros_like(carry)

    buf[0:halo, :] = jnp.where(cc > 0, prev_ref[...], 0.0)
    buf[halo:halo + ts, :] = xa_ref[...]
    buf[halo + ts:2 * halo + ts, :] = jnp.where(cc < nc - 1, next_ref[...], 0.0)
    left = CONV_WIDTH // 2
    xc = cb_ref[...]
    for k in range(CONV_WIDTH):
        xc = xc + buf[halo - left + k:halo - left + k + ts, :] * cw_ref[k:k + 1, :]

    xb = xc.astype(BF16)
    r = jax.nn.sigmoid(jnp.dot(xb, wa_ref[0], preferred_element_type=F32) + ba_ref[...])
    i = jax.nn.sigmoid(jnp.dot(xb, wx_ref[0], preferred_element_type=F32) + bx_ref[...])
    log_a = (LRU_C * r) * _log_sigmoid(lam_ref[...])
    a = jnp.exp(log_a)
    th = jnp.tanh(log_a)
    mult = jnp.sqrt(-2.0 * th / (1.0 - th))
    row = lax.broadcasted_iota(jnp.int32, (ts, 1), 0) + cc * ts
    mult = jnp.where(row == (seq - 1 if reverse else 0), 1.0, mult)
    b = mult * (i * xc)

    for j in range(SUBLANES):
        for l in range(nl):
            a_s[l, j * pitch:j * pitch + seg, :] = a[j * seg:(j + 1) * seg, l * LANES:(l + 1) * LANES]
            b_s[l, j * pitch:j * pitch + seg, :] = b[j * seg:(j + 1) * seg, l * LANES:(l + 1) * LANES]
    hloc = [jnp.zeros((SUBLANES, LANES), F32) for _ in range(nl)]
    prod = [jnp.ones((SUBLANES, LANES), F32) for _ in range(nl)]
    for t in (range(seg - 1, -1, -1) if reverse else range(seg)):
        for l in range(nl):
            at = a_s[l, pl.ds(t, SUBLANES, stride=pitch), :]
            bt = b_s[l, pl.ds(t, SUBLANES, stride=pitch), :]
            hloc[l] = at * hloc[l] + bt
            prod[l] = at * prod[l]
            h_s[l, pl.ds(t, SUBLANES, stride=pitch), :] = hloc[l]
            p_s[l, pl.ds(t, SUBLANES, stride=pitch), :] = prod[l]
    cj = [carry[:, l * LANES:(l + 1) * LANES] for l in range(nl)]
    for j in (range(SUBLANES - 1, -1, -1) if reverse else range(SUBLANES)):
        rows = slice(j * seg, (j + 1) * seg)
        for l in range(nl):
            lanes = slice(l * LANES, (l + 1) * LANES)
            h = h_s[l, j * pitch:j * pitch + seg, :] + p_s[l, j * pitch:j * pitch + seg, :] * cj[l]
            if combine:
                h = gg_ref[rows, lanes] * (hf_ref[rows, lanes] + h)
            o_ref[rows, lanes] = h.astype(o_ref.dtype)
            cj[l] = hloc[l][j:j + 1, :] + prod[l][j:j + 1, :] * cj[l]
    for l in range(nl):
        carry[:, l * LANES:(l + 1) * LANES] = cj[l]


def _lru(xa, conv_w, conv_b, wa, ba, wx, bx, lam, *, reverse, hf=None, gg=None, out_dtype, name):
    bsz, seq, w = xa.shape
    heads, bw, _ = wa.shape
    ts = _tile(seq, 512, SUBLANES * SUBLANES)
    nc = seq // ts
    nhalo = ts // SUBLANES
    combine = hf is not None

    def cidx(c):
        return (nc - 1 - c) if reverse else c

    blk = lambda b, h, c: (b, cidx(c), h)
    in_specs = [
        pl.BlockSpec((None, ts, bw), blk),
        pl.BlockSpec((None, SUBLANES, bw), lambda b, h, c: (b, jnp.maximum(cidx(c) * nhalo - 1, 0), h)),
        pl.BlockSpec((None, SUBLANES, bw),
                     lambda b, h, c: (b, jnp.minimum((cidx(c) + 1) * nhalo, seq // SUBLANES - 1), h)),
        pl.BlockSpec((CONV_WIDTH, bw), lambda b, h, c: (0, h)),
        pl.BlockSpec((1, bw), lambda b, h, c: (0, h)),
        pl.BlockSpec((1, bw, bw), lambda b, h, c: (h, 0, 0)),
        pl.BlockSpec((1, bw), lambda b, h, c: (0, h)),
        pl.BlockSpec((1, bw, bw), lambda b, h, c: (h, 0, 0)),
        pl.BlockSpec((1, bw), lambda b, h, c: (0, h)),
        pl.BlockSpec((1, bw), lambda b, h, c: (0, h)),
    ]
    args = [xa, xa, xa, conv_w.astype(F32), conv_b.reshape(1, w).astype(F32),
            wa.astype(BF16), ba.reshape(1, w).astype(F32), wx.astype(BF16), bx.reshape(1, w).astype(F32),
            lam.reshape(1, w).astype(F32)]
    if combine:
        in_specs += [pl.BlockSpec((None, ts, bw), blk), pl.BlockSpec((None, ts, bw), blk)]
        args += [hf, gg]
    nl = bw // LANES
    rows = SUBLANES * (ts // SUBLANES + SCAN_PITCH_PAD)
    scan_buf = pltpu.VMEM((nl, rows, LANES), F32)
    kern = functools.partial(_lru_kernel, reverse=reverse, combine=combine, ts=ts, nc=nc, seq=seq)
    return pl.pallas_call(
        kern,
        out_shape=jax.ShapeDtypeStruct((bsz, seq, w), out_dtype),
        grid=(bsz, heads, nc),
        in_specs=in_specs,
        out_specs=pl.BlockSpec((None, ts, bw), blk),
        scratch_shapes=[pltpu.VMEM((ts + 2 * SUBLANES, bw), F32), scan_buf, scan_buf, scan_buf, scan_buf,
                        pltpu.VMEM((1, bw), F32)],
        compiler_params=_params("parallel", "parallel", "arbitrary"),
        name=name,
    )(*args)


def _mla_kernel(q_ref, k_ref, v_ref, o_ref, m_s, l_s, acc_s, *, tk, nk):
    q = q_ref[...]
    m_s[...] = jnp.full_like(m_s, -jnp.inf)
    l_s[...] = jnp.zeros_like(l_s)
    acc_s[...] = jnp.zeros_like(acc_s)

    def body(kk, _):
        off = pl.multiple_of(kk * tk, tk)
        k = k_ref[pl.ds(off, tk), :]
        v = v_ref[pl.ds(off, tk), :]
        s = lax.dot_general(q, k, (((1,), (1,)), ((), ())), preferred_element_type=F32)
        m_prev = m_s[...]
        m_new = jnp.maximum(m_prev, jnp.max(s, axis=-1, keepdims=True))
        alpha = jnp.exp(m_prev - m_new)
        p = jnp.exp(s - m_new)
        l_s[...] = alpha * l_s[...] + jnp.sum(p, axis=-1, keepdims=True)
        acc_s[...] = alpha * acc_s[...] + jnp.dot(p.astype(BF16), v, preferred_element_type=F32)
        m_s[...] = m_new
        return 0

    lax.fori_loop(0, nk, body, 0)
    o_ref[...] = (acc_s[...] / l_s[...]).astype(o_ref.dtype)


def _mla_attention(q, k, v, bsz, seq):
    heads = q.shape[1] // QK_PAD
    tq = _tile(seq, 512, SUBLANES)
    tk = _tile(seq, 512, LANES)
    nq = seq // tq
    kern = functools.partial(_mla_kernel, tk=tk, nk=seq // tk)
    return pl.pallas_call(
        kern,
        out_shape=jax.ShapeDtypeStruct((bsz * seq, heads * V_HEAD), BF16),
        grid=(bsz, heads, nq),
        in_specs=[pl.BlockSpec((tq, QK_PAD), lambda b, h, i: (b * nq + i, h)),
                  pl.BlockSpec((seq, QK_PAD), lambda b, h, i: (b, h)),
                  pl.BlockSpec((seq, V_HEAD), lambda b, h, i: (b, h))],
        out_specs=pl.BlockSpec((tq, V_HEAD), lambda b, h, i: (b * nq + i, h)),
        scratch_shapes=[pltpu.VMEM((tq, 1), F32), pltpu.VMEM((tq, 1), F32), pltpu.VMEM((tq, V_HEAD), F32)],
        compiler_params=_params("parallel", "parallel", "arbitrary"),
        name="mla_attention",
    )(q, k, v)


def _mem_attn_kernel(q_ref, kv_ref, o_ref, *, heads, hd):
    for h in range(heads):
        q = q_ref[:, h * hd:(h + 1) * hd]
        k = kv_ref[:, 2 * h * hd:(2 * h + 1) * hd]
        v = kv_ref[:, (2 * h + 1) * hd:(2 * h + 2) * hd]
        s = lax.dot_general(q, k, (((1,), (1,)), ((), ())), preferred_element_type=F32)
        p = jnp.exp(s - jnp.max(s, axis=-1, keepdims=True))
        l = jnp.sum(p, axis=-1, keepdims=True)
        o = jnp.dot(p.astype(BF16), v, preferred_element_type=F32) / l
        o_ref[:, h * hd:(h + 1) * hd] = o.astype(o_ref.dtype)


def _mem_attention(q, kv, bsz, seq, mem_len):
    width = q.shape[1]
    hd = width // MEM_HEADS
    tq = _tile(seq, 512, SUBLANES)
    nq = seq // tq
    kern = functools.partial(_mem_attn_kernel, heads=MEM_HEADS, hd=hd)
    return pl.pallas_call(
        kern,
        out_shape=jax.ShapeDtypeStruct((bsz * seq, width), BF16),
        grid=(bsz, nq),
        in_specs=[pl.BlockSpec((tq, width), lambda b, i: (b * nq + i, 0)),
                  pl.BlockSpec((mem_len, 2 * width), lambda b, i: (b, 0))],
        out_specs=pl.BlockSpec((tq, width), lambda b, i: (b * nq + i, 0)),
        compiler_params=_params("parallel", "arbitrary"),
        name="mem_attention",
    )(q, kv)


def _layer(x, mem, tables, bsz, seq, p):
    t, d = x.shape
    w_in = p["w_in"]
    lru_w = p["conv_w"].shape[1]
    q_lora = p["q_a_norm"].shape[0]
    kv_lora = p["kv_a_norm"].shape[0]
    cos, sin_lo, sin_hi = tables
    tm = _tile(t, 1024, SUBLANES)

    o_xa, o_ga, o_cq, o_ckv, o_kr = 0, lru_w, 2 * lru_w, 2 * lru_w + q_lora, 2 * lru_w + q_lora + kv_lora
    o_gate = o_kr + QK_ROPE
    w_xa = w_in[:, o_xa:o_ga].astype(BF16)
    w_ga = w_in[:, o_ga:o_cq].astype(BF16)
    w_cq = w_in[:, o_cq:o_ckv].astype(BF16)
    w_ckv = jnp.pad(w_in[:, o_ckv:o_gate], ((0, 0), (0, LANES - QK_ROPE))).astype(BF16)
    w_gates = w_in[:, o_gate:].astype(BF16)

    h1 = _rmsnorm(x, p["norm_mix"], BF16, "norm_mix")
    tn = _tile(lru_w, 512, LANES)
    (xa,) = _mm([h1], [w_xa], [0], [(tn, F32)], _ep_plain, tm=tm, tn=tn, name="proj_xa")
    (gg,) = _mm([h1], [w_ga], [0], [(tn, F32)], _ep_gelu, tm=tm, tn=tn, name="proj_ga")
    tn = _tile(2 * d, 512, LANES)
    (gates,) = _mm([h1], [w_gates], [0], [(tn, F32)], _ep_sigmoid, tm=tm, tn=tn, name="proj_gates")
    (cqn,) = _mm([h1], [w_cq], [0], [(q_lora, BF16)], _ep_rms, tm=tm, tn=q_lora,
                 col_ops=[p["q_a_norm"].reshape(1, q_lora).astype(F32)], name="proj_cq")
    g_kv = jnp.pad(p["kv_a_norm"].astype(F32), (0, LANES)).reshape(1, kv_lora + LANES)
    ckvn, k_rope = _mm([h1], [w_ckv], [0], [(kv_lora, BF16), (LANES, BF16)],
                       functools.partial(_ep_ckv, kv_lora), tm=tm, tn=kv_lora + LANES,
                       row_ops=[cos, sin_lo, sin_hi], col_ops=[g_kv], name="proj_ckv")

    xa3 = xa.reshape(bsz, seq, lru_w)
    h_f = _lru(xa3, p["conv_w"], p["conv_b"], p["lru_wa_f"], p["lru_ba_f"], p["lru_wx_f"], p["lru_bx_f"],
               p["lru_lam_f"], reverse=False, out_dtype=F32, name="lru_fwd")
    y_a = _lru(xa3, p["conv_w"], p["conv_b"], p["lru_wa_b"], p["lru_ba_b"], p["lru_wx_b"], p["lru_bx_b"],
               p["lru_lam_b"], reverse=True, hf=h_f, gg=gg.reshape(bsz, seq, lru_w), out_dtype=BF16,
               name="lru_bwd").reshape(t, lru_w)

    heads = MLA_HEADS
    scale = float(QK_NOPE + QK_ROPE) ** -0.5
    w_uq = jnp.pad(p["w_uq"].reshape(q_lora, heads, QK_NOPE + QK_ROPE),
                   ((0, 0), (0, 0), (0, QK_PAD - QK_NOPE - QK_ROPE))).reshape(q_lora, heads * QK_PAD).astype(BF16)
    w_ukv = p["w_ukv"].reshape(kv_lora, heads, QK_NOPE + V_HEAD)
    w_uk = w_ukv[:, :, :QK_NOPE].reshape(kv_lora, heads * QK_NOPE).astype(BF16)
    w_uv = w_ukv[:, :, QK_NOPE:].reshape(kv_lora, heads * V_HEAD).astype(BF16)
    hpt = min(4, heads)
    (q,) = _mm([cqn], [w_uq], [0], [(hpt * QK_PAD, BF16)], functools.partial(_ep_q, scale), tm=tm,
               tn=hpt * QK_PAD, row_ops=[cos, sin_lo, sin_hi], name="proj_q")
    (k,) = _mm([ckvn], [w_uk], [0], [(hpt * QK_PAD, BF16)], _ep_k, tm=tm, tn=hpt * QK_NOPE,
               row_ops=[k_rope], name="proj_k")
    (v,) = _mm([ckvn], [w_uv], [0], [(hpt * V_HEAD, BF16)], _ep_plain, tm=tm, tn=hpt * V_HEAD, name="proj_v")
    y_b = _mla_attention(q, k, v, bsz, seq)

    tn = _tile(d, 512, LANES)
    tm2 = _tile(t, 512, SUBLANES)
    (merged,) = _mm([y_a, y_b], [p["w_proj_lru"].astype(BF16), p["w_proj_mla"].astype(BF16)], [0, 1],
                    [(tn, BF16)], _ep_merge, tm=tm2, tn=tn, tile_ops=[(gates, 0), (gates, d // tn)],
                    name="proj_merge")
    (x1,) = _mm([merged], [p["w_out"].astype(BF16)], [0], [(tn, F32)], _ep_residual, tm=tm, tn=tn,
                tile_ops=[(x, 0)], name="proj_out")

    mem_w = p["w_mem_q"].shape[1]
    mem_len = mem.shape[0] // bsz
    hq = _rmsnorm(x1, p["norm_mem_x"], BF16, "norm_mem_x")
    hm = _rmsnorm(mem, p["norm_mem_kv"], BF16, "norm_mem_kv")
    tnm = _tile(mem_w, 512, LANES)
    (mq,) = _mm([hq], [p["w_mem_q"].astype(BF16)], [0], [(tnm, BF16)],
                functools.partial(_ep_scale, float(mem_w // MEM_HEADS) ** -0.5), tm=tm, tn=tnm, name="mem_q")
    (mkv,) = _mm([hm], [p["w_mem_kv"].astype(BF16)], [0], [(tnm, BF16)], _ep_plain,
                 tm=_tile(mem.shape[0], 512, SUBLANES), tn=tnm, name="mem_kv")
    mo = _mem_attention(mq, mkv, bsz, seq, mem_len)
    (x2,) = _mm([mo], [p["w_mem_o"].astype(BF16)], [0], [(tn, F32)], _ep_residual, tm=tm, tn=tn,
                tile_ops=[(x1, 0)], name="mem_o")

    d_ff = p["w_ffn_gate"].shape[1]
    tnf = 512 if d_ff >= 512 else LANES
    d_ffp = -(-d_ff // (2 * tnf)) * (2 * tnf)
    w_g = jnp.pad(p["w_ffn_gate"], ((0, 0), (0, d_ffp - d_ff))).astype(BF16)
    w_u = jnp.pad(p["w_ffn_up"], ((0, 0), (0, d_ffp - d_ff))).astype(BF16)
    w_d = jnp.pad(p["w_ffn_down"], ((0, d_ffp - d_ff), (0, 0))).astype(BF16)
    hf = _rmsnorm(x2, p["norm_ffn"], BF16, "norm_ffn")
    (act,) = _mm([hf], [w_g, w_u], [0, 0], [(tnf, BF16)], _ep_swiglu, tm=tm, tn=tnf, name="ffn_up")
    x3 = _mm_kgrid_residual(act, w_d, x2, tm=tm, tn=_tile(d, 1024, LANES), tk=d_ffp // 4 if d_ffp >= 2048 else d_ffp,
                            name="ffn_down")
    return x3


def kernel(x, mem, positions, norm_mix, w_in, conv_w, conv_b, lru_wa_f, lru_ba_f, lru_wx_f, lru_bx_f, lru_lam_f,
           lru_wa_b, lru_ba_b, lru_wx_b, lru_bx_b, lru_lam_b, q_a_norm, w_uq, kv_a_norm, w_ukv, w_proj_lru,
           w_proj_mla, w_out, norm_mem_x, norm_mem_kv, w_mem_q, w_mem_kv, w_mem_o, norm_ffn, w_ffn_gate,
           w_ffn_up, w_ffn_down, norm_final):
    bsz, seq, d = x.shape
    stacked = dict(norm_mix=norm_mix, w_in=w_in, conv_w=conv_w, conv_b=conv_b, lru_wa_f=lru_wa_f,
                   lru_ba_f=lru_ba_f, lru_wx_f=lru_wx_f, lru_bx_f=lru_bx_f, lru_lam_f=lru_lam_f,
                   lru_wa_b=lru_wa_b, lru_ba_b=lru_ba_b, lru_wx_b=lru_wx_b, lru_bx_b=lru_bx_b,
                   lru_lam_b=lru_lam_b, q_a_norm=q_a_norm, w_uq=w_uq, kv_a_norm=kv_a_norm, w_ukv=w_ukv,
                   w_proj_lru=w_proj_lru, w_proj_mla=w_proj_mla, w_out=w_out, norm_mem_x=norm_mem_x,
                   norm_mem_kv=norm_mem_kv, w_mem_q=w_mem_q, w_mem_kv=w_mem_kv, w_mem_o=w_mem_o,
                   norm_ffn=norm_ffn, w_ffn_gate=w_ffn_gate, w_ffn_up=w_ffn_up, w_ffn_down=w_ffn_down)
    tables = _rope_tables(positions)
    xf = x.reshape(bsz * seq, d)
    memf = mem.reshape(-1, d)
    for layer in range(norm_mix.shape[0]):
        xf = _layer(xf, memf, tables, bsz, seq, {k: v[layer] for k, v in stacked.items()})
    out = _rmsnorm(xf, norm_final, x.dtype, "norm_final")
    return out.reshape(bsz, seq, d)
```

```python
import functools

import jax
import jax.numpy as jnp
from jax import lax
from jax.experimental import pallas as pl
from jax.experimental.pallas import tpu as pltpu

F32 = jnp.float32
BF16 = jnp.bfloat16

EPS = 1e-6
CONV_WIDTH = 4
LRU_C = 8.0
MLA_HEADS = 32
QK_NOPE = 128
QK_ROPE = 64
V_HEAD = 128
ROPE_THETA = 10000.0
MEM_HEADS = 4
LOG2_E = 1.4426950408889634

LANES = 128
SUBLANES = 8
VMEM_LIMIT_BYTES = 56 * 1024 * 1024

QK_PAD = 2 * LANES
SCAN_PITCH_PAD = 4


def _tile(n, pref, align):
    if n <= pref:
        return n
    t = (pref // align) * align
    while t > align and n % t:
        t -= align
    assert n % t == 0, (n, pref, align)
    return t


def _params(*sem):
    return pltpu.CompilerParams(dimension_semantics=sem, vmem_limit_bytes=VMEM_LIMIT_BYTES)


def _rms(x, g):
    inv = lax.rsqrt(jnp.mean(x * x, axis=-1, keepdims=True) + EPS)
    return x * inv * g


def _rmsnorm_kernel(x_ref, g_ref, o_ref):
    o_ref[...] = _rms(x_ref[...].astype(F32), g_ref[...]).astype(o_ref.dtype)


def _rmsnorm(x, g, out_dtype, name):
    m, d = x.shape
    tm = _tile(m, 256, SUBLANES)
    return pl.pallas_call(
        _rmsnorm_kernel,
        out_shape=jax.ShapeDtypeStruct((m, d), out_dtype),
        grid=(m // tm,),
        in_specs=[pl.BlockSpec((tm, d), lambda i: (i, 0)), pl.BlockSpec((1, d), lambda i: (0, 0))],
        out_specs=pl.BlockSpec((tm, d), lambda i: (i, 0)),
        compiler_params=_params("parallel"),
        name=name,
    )(x, g.reshape(1, d).astype(F32))


def _mm_kernel(*refs, n_a, pairs, n_tile, n_row, n_col, epilogue):
    a_refs = refs[:n_a]
    w_refs = refs[n_a:n_a + len(pairs)]
    p = n_a + len(pairs)
    tile_refs = refs[p:p + n_tile]
    row_refs = refs[p + n_tile:p + n_tile + n_row]
    col_refs = refs[p + n_tile + n_row:p + n_tile + n_row + n_col]
    o_refs = refs[p + n_tile + n_row + n_col:]
    accs = [jnp.dot(a_refs[ai][...], w_ref[...], preferred_element_type=F32)
            for ai, w_ref in zip(pairs, w_refs)]
    epilogue(accs, tile_refs, row_refs, col_refs, o_refs)


def _mm(a_ops, w_ops, pairs, outs, epilogue, *, tm, tn, tile_ops=(), row_ops=(), col_ops=(), name):
    m = a_ops[0].shape[0]
    n = w_ops[0].shape[1]
    assert m % tm == 0 and n % tn == 0, (m, n, tm, tn)
    in_specs = [pl.BlockSpec((tm, a.shape[1]), lambda i, j: (i, 0)) for a in a_ops]
    in_specs += [pl.BlockSpec((w.shape[0], tn), lambda i, j: (0, j)) for w in w_ops]
    in_specs += [pl.BlockSpec((tm, tn), functools.partial(lambda off, i, j: (i, j + off), off))
                 for _, off in tile_ops]
    in_specs += [pl.BlockSpec((tm, r.shape[1]), lambda i, j: (i, 0)) for r in row_ops]
    in_specs += [pl.BlockSpec((1, tn), lambda i, j: (0, j)) for _ in col_ops]
    out_shape = [jax.ShapeDtypeStruct((m, (n // tn) * w), dt) for w, dt in outs]
    out_specs = [pl.BlockSpec((tm, w), lambda i, j: (i, j)) for w, _ in outs]
    kern = functools.partial(_mm_kernel, n_a=len(a_ops), pairs=tuple(pairs), n_tile=len(tile_ops),
                             n_row=len(row_ops), n_col=len(col_ops), epilogue=epilogue)
    res = pl.pallas_call(
        kern,
        out_shape=out_shape,
        grid=(m // tm, n // tn),
        in_specs=in_specs,
        out_specs=out_specs,
        compiler_params=_params("parallel", "arbitrary"),
        name=name,
    )(*a_ops, *w_ops, *[t for t, _ in tile_ops], *row_ops, *col_ops)
    return res


def _ep_plain(accs, tiles, rows, cols, outs):
    outs[0][...] = accs[0].astype(outs[0].dtype)


def _ep_gelu(accs, tiles, rows, cols, outs):
    outs[0][...] = jax.nn.gelu(accs[0]).astype(outs[0].dtype)


def _ep_sigmoid(accs, tiles, rows, cols, outs):
    outs[0][...] = jax.nn.sigmoid(accs[0]).astype(outs[0].dtype)


def _ep_rms(accs, tiles, rows, cols, outs):
    outs[0][...] = _rms(accs[0], cols[0][...]).astype(outs[0].dtype)


def _ep_scale(scale, accs, tiles, rows, cols, outs):
    outs[0][...] = (accs[0] * scale).astype(outs[0].dtype)


def _ep_residual(accs, tiles, rows, cols, outs):
    outs[0][...] = (tiles[0][...] + accs[0]).astype(outs[0].dtype)


def _ep_merge(accs, tiles, rows, cols, outs):
    outs[0][...] = (tiles[0][...] * accs[0] + tiles[1][...] * accs[1]).astype(outs[0].dtype)


def _ep_swiglu(accs, tiles, rows, cols, outs):
    outs[0][...] = (jax.nn.silu(accs[0]) * accs[1]).astype(outs[0].dtype)


def _rope(x, cos, sin_lo, sin_hi):
    half = QK_ROPE // 2
    return x * cos + pltpu.roll(x, half, 1) * sin_hi + pltpu.roll(x, LANES - half, 1) * sin_lo


def _ep_ckv(kv_lora, accs, tiles, rows, cols, outs):
    acc = accs[0]
    outs[0][...] = _rms(acc[:, :kv_lora], cols[0][:, :kv_lora]).astype(outs[0].dtype)
    kr = _rope(acc[:, kv_lora:kv_lora + LANES], rows[0][...], rows[1][...], rows[2][...])
    outs[1][...] = kr.astype(outs[1].dtype)


def _ep_q(scale, accs, tiles, rows, cols, outs):
    acc = accs[0]
    cos, sin_lo, sin_hi = rows[0][...], rows[1][...], rows[2][...]
    for h in range(acc.shape[1] // QK_PAD):
        c0 = h * QK_PAD
        outs[0][:, c0:c0 + QK_NOPE] = (acc[:, c0:c0 + QK_NOPE] * scale).astype(outs[0].dtype)
        rp = _rope(acc[:, c0 + QK_NOPE:c0 + QK_PAD], cos, sin_lo, sin_hi) * scale
        outs[0][:, c0 + QK_NOPE:c0 + QK_PAD] = rp.astype(outs[0].dtype)


def _ep_k(accs, tiles, rows, cols, outs):
    acc = accs[0]
    kr = rows[0][...]
    for h in range(acc.shape[1] // QK_NOPE):
        outs[0][:, h * QK_PAD:h * QK_PAD + QK_NOPE] = acc[:, h * QK_NOPE:(h + 1) * QK_NOPE].astype(outs[0].dtype)
        outs[0][:, h * QK_PAD + QK_NOPE:(h + 1) * QK_PAD] = kr


def _ep_v(accs, tiles, rows, cols, outs):
    acc = accs[0]
    for h in range(acc.shape[1] // V_HEAD):
        outs[0][:, 2 * h * V_HEAD:(2 * h + 1) * V_HEAD] = acc[:, h * V_HEAD:(h + 1) * V_HEAD].astype(outs[0].dtype)
        outs[0][:, (2 * h + 1) * V_HEAD:(2 * h + 2) * V_HEAD] = jnp.ones((acc.shape[0], V_HEAD), outs[0].dtype)


def _mm_kgrid_kernel(a_ref, w_ref, r_ref, o_ref, acc_ref):
    k = pl.program_id(2)

    @pl.when(k == 0)
    def _():
        acc_ref[...] = jnp.zeros_like(acc_ref)

    acc_ref[...] += jnp.dot(a_ref[...], w_ref[...], preferred_element_type=F32)

    @pl.when(k == pl.num_programs(2) - 1)
    def _():
        o_ref[...] = (r_ref[...] + acc_ref[...]).astype(o_ref.dtype)


def _mm_kgrid_residual(a, w, res, *, tm, tn, tk, name):
    m, kdim = a.shape
    n = w.shape[1]
    assert m % tm == 0 and n % tn == 0 and kdim % tk == 0
    return pl.pallas_call(
        _mm_kgrid_kernel,
        out_shape=jax.ShapeDtypeStruct((m, n), res.dtype),
        grid=(m // tm, n // tn, kdim // tk),
        in_specs=[pl.BlockSpec((tm, tk), lambda i, j, k: (i, k)),
                  pl.BlockSpec((tk, tn), lambda i, j, k: (k, j)),
                  pl.BlockSpec((tm, tn), lambda i, j, k: (i, j))],
        out_specs=pl.BlockSpec((tm, tn), lambda i, j, k: (i, j)),
        scratch_shapes=[pltpu.VMEM((tm, tn), F32)],
        compiler_params=_params("parallel", "parallel", "arbitrary"),
        name=name,
    )(a, w, res)


def _rope_table_kernel(pos_ref, f_ref, cos_ref, slo_ref, shi_ref):
    ang = pos_ref[...] * f_ref[...]
    lane = lax.broadcasted_iota(jnp.int32, ang.shape, 1)
    half = QK_ROPE // 2
    c = jnp.cos(ang)
    s = jnp.sin(ang)
    cos_ref[...] = jnp.where(lane < QK_ROPE, c, 0.0)
    slo_ref[...] = jnp.where(lane < half, -s, 0.0)
    shi_ref[...] = jnp.where((lane >= half) & (lane < QK_ROPE), s, 0.0)


def _rope_tables(positions):
    t = positions.size
    inv_freq = ROPE_THETA ** (-jnp.arange(0, QK_ROPE, 2, dtype=F32) / QK_ROPE)
    f = jnp.concatenate([inv_freq, inv_freq, jnp.zeros((LANES - QK_ROPE,), F32)]).reshape(1, LANES)
    pos = positions.astype(F32).reshape(t, 1)
    tm = _tile(t, 1024, SUBLANES)
    shp = jax.ShapeDtypeStruct((t, LANES), F32)
    return pl.pallas_call(
        _rope_table_kernel,
        out_shape=[shp, shp, shp],
        grid=(t // tm,),
        in_specs=[pl.BlockSpec((tm, 1), lambda i: (i, 0)), pl.BlockSpec((1, LANES), lambda i: (0, 0))],
        out_specs=[pl.BlockSpec((tm, LANES), lambda i: (i, 0))] * 3,
        compiler_params=_params("parallel"),
        name="rope_tables",
    )(pos, f)


def _log_sigmoid(x):
    return jnp.minimum(x, 0.0) - jnp.log1p(jnp.exp(-jnp.abs(x)))


def _lru_kernel(*refs, reverse, combine, ts, nc, seq):
    (xa_ref, prev_ref, next_ref, cw_ref, cb_ref, wa_ref, ba_ref, wx_ref, bx_ref, lam_ref) = refs[:10]
    if combine:
        hf_ref, gg_ref = refs[10:12]
        refs = refs[12:]
    else:
        refs = refs[10:]
    o_ref, buf, a_s, b_s, h_s, p_s, carry = refs
    bw = xa_ref.shape[-1]
    nl = bw // LANES
    seg = ts // SUBLANES
    pitch = seg + SCAN_PITCH_PAD
    halo = SUBLANES

    c = pl.program_id(2)
    cc = (nc - 1 - c) if reverse else c

    @pl.when(c == 0)
    def _():
        carry[...] = jnp.zeros_like(carry)

    buf[0:halo, :] = jnp.where(cc > 0, prev_ref[...], 0.0)
    buf[halo:halo + ts, :] = xa_ref[...]
    buf[halo + ts:2 * halo + ts, :] = jnp.where(cc < nc - 1, next_ref[...], 0.0)
    left = CONV_WIDTH // 2
    xc = cb_ref[...]
    for k in range(CONV_WIDTH):
        xc = xc + buf[halo - left + k:halo - left + k + ts, :] * cw_ref[k:k + 1, :]

    xb = xc.astype(BF16)
    r = jax.nn.sigmoid(jnp.dot(xb, wa_ref[0], preferred_element_type=F32) + ba_ref[...])
    i = jax.nn.sigmoid(jnp.dot(xb, wx_ref[0], preferred_element_type=F32) + bx_ref[...])
    log_a = (LRU_C * r) * _log_sigmoid(lam_ref[...])
    a = jnp.exp(log_a)
    th = jnp.tanh(log_a)
    mult = jnp.sqrt(-2.0 * th / (1.0 - th))
    row = lax.broadcasted_iota(jnp.int32, (ts, 1), 0) + cc * ts
    mult = jnp.where(row == (seq - 1 if reverse else 0), 1.0, mult)
    b = mult * (i * xc)

    for j in range(SUBLANES):
        for l in range(nl):
            a_s[l, j * pitch:j * pitch + seg, :] = a[j * seg:(j + 1) * seg, l * LANES:(l + 1) * LANES]
            b_s[l, j * pitch:j * pitch + seg, :] = b[j * seg:(j + 1) * seg, l * LANES:(l + 1) * LANES]
    hloc = [jnp.zeros((SUBLANES, LANES), F32) for _ in range(nl)]
    prod = [jnp.ones((SUBLANES, LANES), F32) for _ in range(nl)]
    for t in (range(seg - 1, -1, -1) if reverse else range(seg)):
        for l in range(nl):
            at = a_s[l, pl.ds(t, SUBLANES, stride=pitch), :]
            bt = b_s[l, pl.ds(t, SUBLANES, stride=pitch), :]
            hloc[l] = at * hloc[l] + bt
            prod[l] = at * prod[l]
            h_s[l, pl.ds(t, SUBLANES, stride=pitch), :] = hloc[l]
            p_s[l, pl.ds(t, SUBLANES, stride=pitch), :] = prod[l]
    cj = [carry[:, l * LANES:(l + 1) * LANES] for l in range(nl)]
    for j in (range(SUBLANES - 1, -1, -1) if reverse else range(SUBLANES)):
        rows = slice(j * seg, (j + 1) * seg)
        for l in range(nl):
            lanes = slice(l * LANES, (l + 1) * LANES)
            h = h_s[l, j * pitch:j * pitch + seg, :] + p_s[l, j * pitch:j * pitch + seg, :] * cj[l]
            if combine:
                h = gg_ref[rows, lanes] * (hf_ref[rows, lanes] + h)
            o_ref[rows, lanes] = h.astype(o_ref.dtype)
            cj[l] = hloc[l][j:j + 1, :] + prod[l][j:j + 1, :] * cj[l]
    for l in range(nl):
        carry[:, l * LANES:(l + 1) * LANES] = cj[l]


def _lru(xa, conv_w, conv_b, wa, ba, wx, bx, lam, *, reverse, hf=None, gg=None, out_dtype, name):
    bsz, seq, w = xa.shape
    heads, bw, _ = wa.shape
    ts = _tile(seq, 512, SUBLANES * SUBLANES)
    nc = seq // ts
    nhalo = ts // SUBLANES
    combine = hf is not None

    def cidx(c):
        return (nc - 1 - c) if reverse else c

    blk = lambda b, h, c: (b, cidx(c), h)
    in_specs = [
        pl.BlockSpec((None, ts, bw), blk),
        pl.BlockSpec((None, SUBLANES, bw), lambda b, h, c: (b, jnp.maximum(cidx(c) * nhalo - 1, 0), h)),
        pl.BlockSpec((None, SUBLANES, bw),
                     lambda b, h, c: (b, jnp.minimum((cidx(c) + 1) * nhalo, seq // SUBLANES - 1), h)),
        pl.BlockSpec((CONV_WIDTH, bw), lambda b, h, c: (0, h)),
        pl.BlockSpec((1, bw), lambda b, h, c: (0, h)),
        pl.BlockSpec((1, bw, bw), lambda b, h, c: (h, 0, 0)),
        pl.BlockSpec((1, bw), lambda b, h, c: (0, h)),
        pl.BlockSpec((1, bw, bw), lambda b, h, c: (h, 0, 0)),
        pl.BlockSpec((1, bw), lambda b, h, c: (0, h)),
        pl.BlockSpec((1, bw), lambda b, h, c: (0, h)),
    ]
    args = [xa, xa, xa, conv_w.astype(F32), conv_b.reshape(1, w).astype(F32),
            wa.astype(BF16), ba.reshape(1, w).astype(F32), wx.astype(BF16), bx.reshape(1, w).astype(F32),
            lam.reshape(1, w).astype(F32)]
    if combine:
        in_specs += [pl.BlockSpec((None, ts, bw), blk), pl.BlockSpec((None, ts, bw), blk)]
        args += [hf, gg]
    nl = bw // LANES
    rows = SUBLANES * (ts // SUBLANES + SCAN_PITCH_PAD)
    scan_buf = pltpu.VMEM((nl, rows, LANES), F32)
    kern = functools.partial(_lru_kernel, reverse=reverse, combine=combine, ts=ts, nc=nc, seq=seq)
    return pl.pallas_call(
        kern,
        out_shape=jax.ShapeDtypeStruct((bsz, seq, w), out_dtype),
        grid=(bsz, heads, nc),
        in_specs=in_specs,
        out_specs=pl.BlockSpec((None, ts, bw), blk),
        scratch_shapes=[pltpu.VMEM((ts + 2 * SUBLANES, bw), F32), scan_buf, scan_buf, scan_buf, scan_buf,
                        pltpu.VMEM((1, bw), F32)],
        compiler_params=_params("parallel", "parallel", "arbitrary"),
        name=name,
    )(*args)


def _mla_kernel(q_ref, k_ref, v_ref, o_ref, m_s, acc_s, s0_s, s1_s, *, tq, tk):
    seq = k_ref.shape[0]
    nq, nk = seq // tq, seq // tk
    reps = tk // LANES
    s_bufs = (s0_s, s1_s)

    def scores(qi, kj):
        qoff = pl.multiple_of(qi * tq, tq)
        return lax.dot_general(q_ref[pl.ds(qoff, tq), :], k_ref[kj * tk:(kj + 1) * tk, :],
                               (((1,), (1,)), ((), ())), preferred_element_type=F32)

    def consume(s_ref, kj):
        s = s_ref[...]
        m_blk = jnp.max(s, axis=-1, keepdims=True)
        m_new = jnp.broadcast_to(m_blk, (tq, LANES)) if kj == 0 else jnp.maximum(m_s[...], m_blk)
        p = jnp.exp2(s - jnp.concatenate([m_new] * reps, axis=1))
        pv = jnp.dot(p.astype(BF16), v_ref[kj * tk:(kj + 1) * tk, :], preferred_element_type=F32)
        if kj == 0:
            acc_s[...] = pv
        else:
            alpha = jnp.exp2(m_s[...] - m_new)
            acc_s[...] = jnp.concatenate([alpha, alpha], axis=1) * acc_s[...] + pv
        m_s[...] = m_new

    s0_s[...] = scores(0, 0)

    def body(qi, _):
        for kj in range(nk):
            nxt = s_bufs[(kj + 1) % 2]
            if kj + 1 < nk:
                nxt[...] = scores(qi, kj + 1)
            else:
                nxt[...] = scores(jnp.minimum(qi + 1, nq - 1), 0)
            consume(s_bufs[kj % 2], kj)
        qoff = pl.multiple_of(qi * tq, tq)
        o_ref[pl.ds(qoff, tq), :] = (acc_s[:, :V_HEAD] / acc_s[:, V_HEAD:]).astype(o_ref.dtype)
        return 0

    lax.fori_loop(0, nq, body, 0)


def _mla_attention(q, k, v, bsz, seq):
    heads = q.shape[1] // QK_PAD
    tq = _tile(seq, 512, SUBLANES)
    tk = _tile(seq // 2, 1024, LANES)
    assert (seq // tk) % 2 == 0
    kern = functools.partial(_mla_kernel, tq=tq, tk=tk)
    return pl.pallas_call(
        kern,
        out_shape=jax.ShapeDtypeStruct((bsz * seq, heads * V_HEAD), BF16),
        grid=(bsz, heads),
        in_specs=[pl.BlockSpec((seq, QK_PAD), lambda b, h: (b, h)),
                  pl.BlockSpec((seq, QK_PAD), lambda b, h: (b, h)),
                  pl.BlockSpec((seq, 2 * V_HEAD), lambda b, h: (b, h))],
        out_specs=pl.BlockSpec((seq, V_HEAD), lambda b, h: (b, h)),
        scratch_shapes=[pltpu.VMEM((tq, LANES), F32), pltpu.VMEM((tq, 2 * V_HEAD), F32),
                        pltpu.VMEM((tq, tk), F32), pltpu.VMEM((tq, tk), F32)],
        compiler_params=_params("parallel", "arbitrary"),
        name="mla_attention",
    )(q, k, v)


def _mem_attn_kernel(q_ref, kv_ref, o_ref, *, heads, hd):
    for h in range(heads):
        q = q_ref[:, h * hd:(h + 1) * hd]
        k = kv_ref[:, 2 * h * hd:(2 * h + 1) * hd]
        v = kv_ref[:, (2 * h + 1) * hd:(2 * h + 2) * hd]
        s = lax.dot_general(q, k, (((1,), (1,)), ((), ())), preferred_element_type=F32)
        p = jnp.exp(s - jnp.max(s, axis=-1, keepdims=True))
        l = jnp.sum(p, axis=-1, keepdims=True)
        o = jnp.dot(p.astype(BF16), v, preferred_element_type=F32) / l
        o_ref[:, h * hd:(h + 1) * hd] = o.astype(o_ref.dtype)


def _mem_attention(q, kv, bsz, seq, mem_len):
    width = q.shape[1]
    hd = width // MEM_HEADS
    tq = _tile(seq, 512, SUBLANES)
    nq = seq // tq
    kern = functools.partial(_mem_attn_kernel, heads=MEM_HEADS, hd=hd)
    return pl.pallas_call(
        kern,
        out_shape=jax.ShapeDtypeStruct((bsz * seq, width), BF16),
        grid=(bsz, nq),
        in_specs=[pl.BlockSpec((tq, width), lambda b, i: (b * nq + i, 0)),
                  pl.BlockSpec((mem_len, 2 * width), lambda b, i: (b, 0))],
        out_specs=pl.BlockSpec((tq, width), lambda b, i: (b * nq + i, 0)),
        compiler_params=_params("parallel", "arbitrary"),
        name="mem_attention",
    )(q, kv)


def _layer(x, mem, tables, bsz, seq, p):
    t, d = x.shape
    w_in = p["w_in"]
    lru_w = p["conv_w"].shape[1]
    q_lora = p["q_a_norm"].shape[0]
    kv_lora = p["kv_a_norm"].shape[0]
    cos, sin_lo, sin_hi = tables
    tm = _tile(t, 1024, SUBLANES)

    o_xa, o_ga, o_cq, o_ckv, o_kr = 0, lru_w, 2 * lru_w, 2 * lru_w + q_lora, 2 * lru_w + q_lora + kv_lora
    o_gate = o_kr + QK_ROPE
    w_xa = w_in[:, o_xa:o_ga].astype(BF16)
    w_ga = w_in[:, o_ga:o_cq].astype(BF16)
    w_cq = w_in[:, o_cq:o_ckv].astype(BF16)
    w_ckv = jnp.pad(w_in[:, o_ckv:o_gate], ((0, 0), (0, LANES - QK_ROPE))).astype(BF16)
    w_gates = w_in[:, o_gate:].astype(BF16)

    h1 = _rmsnorm(x, p["norm_mix"], BF16, "norm_mix")
    tn = _tile(lru_w, 512, LANES)
    (xa,) = _mm([h1], [w_xa], [0], [(tn, F32)], _ep_plain, tm=tm, tn=tn, name="proj_xa")
    (gg,) = _mm([h1], [w_ga], [0], [(tn, F32)], _ep_gelu, tm=tm, tn=tn, name="proj_ga")
    tn = _tile(2 * d, 512, LANES)
    (gates,) = _mm([h1], [w_gates], [0], [(tn, F32)], _ep_sigmoid, tm=tm, tn=tn, name="proj_gates")
    (cqn,) = _mm([h1], [w_cq], [0], [(q_lora, BF16)], _ep_rms, tm=tm, tn=q_lora,
                 col_ops=[p["q_a_norm"].reshape(1, q_lora).astype(F32)], name="proj_cq")
    g_kv = jnp.pad(p["kv_a_norm"].astype(F32), (0, LANES)).reshape(1, kv_lora + LANES)
    ckvn, k_rope = _mm([h1], [w_ckv], [0], [(kv_lora, BF16), (LANES, BF16)],
                       functools.partial(_ep_ckv, kv_lora), tm=tm, tn=kv_lora + LANES,
                       row_ops=[cos, sin_lo, sin_hi], col_ops=[g_kv], name="proj_ckv")

    xa3 = xa.reshape(bsz, seq, lru_w)
    h_f = _lru(xa3, p["conv_w"], p["conv_b"], p["lru_wa_f"], p["lru_ba_f"], p["lru_wx_f"], p["lru_bx_f"],
               p["lru_lam_f"], reverse=False, out_dtype=F32, name="lru_fwd")
    y_a = _lru(xa3, p["conv_w"], p["conv_b"], p["lru_wa_b"], p["lru_ba_b"], p["lru_wx_b"], p["lru_bx_b"],
               p["lru_lam_b"], reverse=True, hf=h_f, gg=gg.reshape(bsz, seq, lru_w), out_dtype=BF16,
               name="lru_bwd").reshape(t, lru_w)

    heads = MLA_HEADS
    scale = float(QK_NOPE + QK_ROPE) ** -0.5 * LOG2_E
    w_uq = jnp.pad(p["w_uq"].reshape(q_lora, heads, QK_NOPE + QK_ROPE),
                   ((0, 0), (0, 0), (0, QK_PAD - QK_NOPE - QK_ROPE))).reshape(q_lora, heads * QK_PAD).astype(BF16)
    w_ukv = p["w_ukv"].reshape(kv_lora, heads, QK_NOPE + V_HEAD)
    w_uk = w_ukv[:, :, :QK_NOPE].reshape(kv_lora, heads * QK_NOPE).astype(BF16)
    w_uv = w_ukv[:, :, QK_NOPE:].reshape(kv_lora, heads * V_HEAD).astype(BF16)
    hpt = min(4, heads)
    (q,) = _mm([cqn], [w_uq], [0], [(hpt * QK_PAD, BF16)], functools.partial(_ep_q, scale), tm=tm,
               tn=hpt * QK_PAD, row_ops=[cos, sin_lo, sin_hi], name="proj_q")
    (k,) = _mm([ckvn], [w_uk], [0], [(hpt * QK_PAD, BF16)], _ep_k, tm=tm, tn=hpt * QK_NOPE,
               row_ops=[k_rope], name="proj_k")
    (v,) = _mm([ckvn], [w_uv], [0], [(2 * hpt * V_HEAD, BF16)], _ep_v, tm=tm, tn=hpt * V_HEAD, name="proj_v")
    y_b = _mla_attention(q, k, v, bsz, seq)

    tn = _tile(d, 512, LANES)
    tm2 = _tile(t, 512, SUBLANES)
    (merged,) = _mm([y_a, y_b], [p["w_proj_lru"].astype(BF16), p["w_proj_mla"].astype(BF16)], [0, 1],
                    [(tn, BF16)], _ep_merge, tm=tm2, tn=tn, tile_ops=[(gates, 0), (gates, d // tn)],
                    name="proj_merge")
    (x1,) = _mm([merged], [p["w_out"].astype(BF16)], [0], [(tn, F32)], _ep_residual, tm=tm, tn=tn,
                tile_ops=[(x, 0)], name="proj_out")

    mem_w = p["w_mem_q"].shape[1]
    mem_len = mem.shape[0] // bsz
    hq = _rmsnorm(x1, p["norm_mem_x"], BF16, "norm_mem_x")
    hm = _rmsnorm(mem, p["norm_mem_kv"], BF16, "norm_mem_kv")
    tnm = _tile(mem_w, 512, LANES)
    (mq,) = _mm([hq], [p["w_mem_q"].astype(BF16)], [0], [(tnm, BF16)],
                functools.partial(_ep_scale, float(mem_w // MEM_HEADS) ** -0.5), tm=tm, tn=tnm, name="mem_q")
    (mkv,) = _mm([hm], [p["w_mem_kv"].astype(BF16)], [0], [(tnm, BF16)], _ep_plain,
                 tm=_tile(mem.shape[0], 512, SUBLANES), tn=tnm, name="mem_kv")
    mo = _mem_attention(mq, mkv, bsz, seq, mem_len)
    (x2,) = _mm([mo], [p["w_mem_o"].astype(BF16)], [0], [(tn, F32)], _ep_residual, tm=tm, tn=tn,
                tile_ops=[(x1, 0)], name="mem_o")

    d_ff = p["w_ffn_gate"].shape[1]
    tnf = 512 if d_ff >= 512 else LANES
    d_ffp = -(-d_ff // (2 * tnf)) * (2 * tnf)
    w_g = jnp.pad(p["w_ffn_gate"].astype(BF16), ((0, 0), (0, d_ffp - d_ff)))
    w_u = jnp.pad(p["w_ffn_up"].astype(BF16), ((0, 0), (0, d_ffp - d_ff)))
    w_d = jnp.pad(p["w_ffn_down"].astype(BF16), ((0, d_ffp - d_ff), (0, 0)))
    hf = _rmsnorm(x2, p["norm_ffn"], BF16, "norm_ffn")
    (act,) = _mm([hf], [w_g, w_u], [0, 0], [(tnf, BF16)], _ep_swiglu, tm=tm, tn=tnf, name="ffn_up")
    x3 = _mm_kgrid_residual(act, w_d, x2, tm=tm, tn=_tile(d, 1024, LANES), tk=d_ffp // 4 if d_ffp >= 2048 else d_ffp,
                            name="ffn_down")
    return x3


def kernel(x, mem, positions, norm_mix, w_in, conv_w, conv_b, lru_wa_f, lru_ba_f, lru_wx_f, lru_bx_f, lru_lam_f,
           lru_wa_b, lru_ba_b, lru_wx_b, lru_bx_b, lru_lam_b, q_a_norm, w_uq, kv_a_norm, w_ukv, w_proj_lru,
           w_proj_mla, w_out, norm_mem_x, norm_mem_kv, w_mem_q, w_mem_kv, w_mem_o, norm_ffn, w_ffn_gate,
           w_ffn_up, w_ffn_down, norm_final):
    bsz, seq, d = x.shape
    stacked = dict(norm_mix=norm_mix, w_in=w_in, conv_w=conv_w, conv_b=conv_b, lru_wa_f=lru_wa_f,
                   lru_ba_f=lru_ba_f, lru_wx_f=lru_wx_f, lru_bx_f=lru_bx_f, lru_lam_f=lru_lam_f,
                   lru_wa_b=lru_wa_b, lru_ba_b=lru_ba_b, lru_wx_b=lru_wx_b, lru_bx_b=lru_bx_b,
                   lru_lam_b=lru_lam_b, q_a_norm=q_a_norm, w_uq=w_uq, kv_a_norm=kv_a_norm, w_ukv=w_ukv,
                   w_proj_lru=w_proj_lru, w_proj_mla=w_proj_mla, w_out=w_out, norm_mem_x=norm_mem_x,
                   norm_mem_kv=norm_mem_kv, w_mem_q=w_mem_q, w_mem_kv=w_mem_kv, w_mem_o=w_mem_o,
                   norm_ffn=norm_ffn, w_ffn_gate=w_ffn_gate, w_ffn_up=w_ffn_up, w_ffn_down=w_ffn_down)
    tables = _rope_tables(positions)
    xf = x.reshape(bsz * seq, d)
    memf = mem.reshape(-1, d)
    for layer in range(norm_mix.shape[0]):
        xf = _layer(xf, memf, tables, bsz, seq, {k: v[layer] for k, v in stacked.items()})
    out = _rmsnorm(xf, norm_final, x.dtype, "norm_final")
    return out.reshape(bsz, seq, d)
```

```python
import functools

import jax
import jax.numpy as jnp
from jax import lax
from jax.experimental import pallas as pl
from jax.experimental.pallas import tpu as pltpu

F32 = jnp.float32
BF16 = jnp.bfloat16

EPS = 1e-6
CONV_WIDTH = 4
LRU_C = 8.0
MLA_HEADS = 32
QK_NOPE = 128
QK_ROPE = 64
V_HEAD = 128
ROPE_THETA = 10000.0
MEM_HEADS = 4
LOG2_E = 1.4426950408889634

LANES = 128
SUBLANES = 8
VMEM_LIMIT_BYTES = 56 * 1024 * 1024

QK_PAD = 2 * LANES
SCAN_PITCH_PAD = 4


def _tile(n, pref, align):
    if n <= pref:
        return n
    t = (pref // align) * align
    while t > align and n % t:
        t -= align
    assert n % t == 0, (n, pref, align)
    return t


def _params(*sem, flags=None):
    return pltpu.CompilerParams(dimension_semantics=sem, vmem_limit_bytes=VMEM_LIMIT_BYTES, flags=flags)


def _rms(x, g):
    inv = lax.rsqrt(jnp.mean(x * x, axis=-1, keepdims=True) + EPS)
    return x * inv * g


def _rmsnorm_kernel(x_ref, g_ref, o_ref):
    o_ref[...] = _rms(x_ref[...].astype(F32), g_ref[...]).astype(o_ref.dtype)


def _rmsnorm(x, g, out_dtype, name):
    m, d = x.shape
    tm = _tile(m, 256, SUBLANES)
    return pl.pallas_call(
        _rmsnorm_kernel,
        out_shape=jax.ShapeDtypeStruct((m, d), out_dtype),
        grid=(m // tm,),
        in_specs=[pl.BlockSpec((tm, d), lambda i: (i, 0)), pl.BlockSpec((1, d), lambda i: (0, 0))],
        out_specs=pl.BlockSpec((tm, d), lambda i: (i, 0)),
        compiler_params=_params("parallel"),
        name=name,
    )(x, g.reshape(1, d).astype(F32))


def _mm_kernel(*refs, n_a, pairs, n_tile, n_row, n_col, n_out, casts, epilogue):
    a_refs = refs[:n_a]
    w_refs = refs[n_a:n_a + len(pairs)]
    p = n_a + len(pairs)
    tile_refs = refs[p:p + n_tile]
    row_refs = refs[p + n_tile:p + n_tile + n_row]
    col_refs = refs[p + n_tile + n_row:p + n_tile + n_row + n_col]
    p += n_tile + n_row + n_col
    cast_in = refs[p:p + len(casts)]
    o_refs = refs[p + len(casts):p + len(casts) + n_out]
    cast_out = refs[p + len(casts) + n_out:]
    accs = [jnp.dot(a_refs[ai][...], w_ref[...], preferred_element_type=F32)
            for ai, w_ref in zip(pairs, w_refs)]
    epilogue(accs, tile_refs, row_refs, col_refs, o_refs)
    step = pl.program_id(0) * pl.num_programs(1) + pl.program_id(1)
    for src, dst, valid_blocks in zip(cast_in, cast_out, casts):
        cols = src.shape[1]
        dst[:, :cols] = jnp.where(step < valid_blocks, src[...], 0.0).astype(dst.dtype)
        if dst.shape[1] > cols:
            dst[:, cols:] = jnp.zeros((dst.shape[0], dst.shape[1] - cols), dst.dtype)


def _cast_rows_per_step(src_rows, out_rows, steps):
    rps = out_rows // steps
    ok = rps * steps == out_rows and rps % (2 * SUBLANES) == 0 and src_rows % rps == 0
    return rps if ok else None


def _mm(a_ops, w_ops, pairs, outs, epilogue, *, tm, tn, tile_ops=(), row_ops=(), col_ops=(), casts=(), name):
    m = a_ops[0].shape[0]
    n = w_ops[0].shape[1]
    assert m % tm == 0 and n % tn == 0, (m, n, tm, tn)
    nj = n // tn
    cast_specs_in, cast_specs_out, cast_shapes, cast_valid = [], [], [], []
    for src, out_rows, out_cols in casts:
        rps = _cast_rows_per_step(src.shape[0], out_rows, (m // tm) * nj)
        valid = src.shape[0] // rps
        cast_specs_in.append(pl.BlockSpec(
            (rps, src.shape[1]), functools.partial(lambda v, i, j: (jnp.minimum(i * nj + j, v - 1), 0), valid)))
        cast_specs_out.append(pl.BlockSpec((rps, out_cols), lambda i, j: (i * nj + j, 0)))
        cast_shapes.append(jax.ShapeDtypeStruct((out_rows, out_cols), BF16))
        cast_valid.append(valid)
    in_specs = [pl.BlockSpec((tm, a.shape[1]), lambda i, j: (i, 0)) for a in a_ops]
    in_specs += [pl.BlockSpec((w.shape[0], tn), lambda i, j: (0, j)) for w in w_ops]
    in_specs += [pl.BlockSpec((tm, tn), functools.partial(lambda off, i, j: (i, j + off), off))
                 for _, off in tile_ops]
    in_specs += [pl.BlockSpec((tm, r.shape[1]), lambda i, j: (i, 0)) for r in row_ops]
    in_specs += [pl.BlockSpec((1, tn), lambda i, j: (0, j)) for _ in col_ops]
    in_specs += cast_specs_in
    out_shape = [jax.ShapeDtypeStruct((m, (n // tn) * w), dt) for w, dt in outs] + cast_shapes
    out_specs = [pl.BlockSpec((tm, w), lambda i, j: (i, j)) for w, _ in outs] + cast_specs_out
    kern = functools.partial(_mm_kernel, n_a=len(a_ops), pairs=tuple(pairs), n_tile=len(tile_ops),
                             n_row=len(row_ops), n_col=len(col_ops), n_out=len(outs),
                             casts=tuple(cast_valid), epilogue=epilogue)
    res = pl.pallas_call(
        kern,
        out_shape=out_shape,
        grid=(m // tm, n // tn),
        in_specs=in_specs,
        out_specs=out_specs,
        compiler_params=_params("parallel", "arbitrary"),
        name=name,
    )(*a_ops, *w_ops, *[t for t, _ in tile_ops], *row_ops, *col_ops, *[c[0] for c in casts])
    return res


def _ep_plain(accs, tiles, rows, cols, outs):
    outs[0][...] = accs[0].astype(outs[0].dtype)


def _ep_gelu(accs, tiles, rows, cols, outs):
    outs[0][...] = jax.nn.gelu(accs[0]).astype(outs[0].dtype)


def _ep_sigmoid(accs, tiles, rows, cols, outs):
    outs[0][...] = jax.nn.sigmoid(accs[0]).astype(outs[0].dtype)


def _ep_rms(accs, tiles, rows, cols, outs):
    outs[0][...] = _rms(accs[0], cols[0][...]).astype(outs[0].dtype)


def _ep_scale(scale, accs, tiles, rows, cols, outs):
    outs[0][...] = (accs[0] * scale).astype(outs[0].dtype)


def _ep_residual(accs, tiles, rows, cols, outs):
    outs[0][...] = (tiles[0][...] + accs[0]).astype(outs[0].dtype)


def _ep_merge(accs, tiles, rows, cols, outs):
    outs[0][...] = (tiles[0][...] * accs[0] + tiles[1][...] * accs[1]).astype(outs[0].dtype)


def _ep_swiglu(accs, tiles, rows, cols, outs):
    outs[0][...] = (jax.nn.silu(accs[0]) * accs[1]).astype(outs[0].dtype)


def _rope(x, cos, sin_lo, sin_hi):
    half = QK_ROPE // 2
    return x * cos + pltpu.roll(x, half, 1) * sin_hi + pltpu.roll(x, LANES - half, 1) * sin_lo


def _ep_ckv(kv_lora, accs, tiles, rows, cols, outs):
    acc = accs[0]
    outs[0][...] = _rms(acc[:, :kv_lora], cols[0][:, :kv_lora]).astype(outs[0].dtype)
    kr = _rope(acc[:, kv_lora:kv_lora + LANES], rows[0][...], rows[1][...], rows[2][...])
    outs[1][...] = kr.astype(outs[1].dtype)


def _ep_q(scale, accs, tiles, rows, cols, outs):
    acc = accs[0]
    cos, sin_lo, sin_hi = rows[0][...], rows[1][...], rows[2][...]
    for h in range(acc.shape[1] // QK_PAD):
        c0 = h * QK_PAD
        outs[0][:, c0:c0 + QK_NOPE] = (acc[:, c0:c0 + QK_NOPE] * scale).astype(outs[0].dtype)
        rp = _rope(acc[:, c0 + QK_NOPE:c0 + QK_PAD], cos, sin_lo, sin_hi) * scale
        outs[0][:, c0 + QK_NOPE:c0 + QK_PAD] = rp.astype(outs[0].dtype)


def _ep_k(accs, tiles, rows, cols, outs):
    acc = accs[0]
    kr = rows[0][...]
    for h in range(acc.shape[1] // QK_NOPE):
        outs[0][:, h * QK_PAD:h * QK_PAD + QK_NOPE] = acc[:, h * QK_NOPE:(h + 1) * QK_NOPE].astype(outs[0].dtype)
        outs[0][:, h * QK_PAD + QK_NOPE:(h + 1) * QK_PAD] = kr


def _ep_v(accs, tiles, rows, cols, outs):
    acc = accs[0]
    for h in range(acc.shape[1] // V_HEAD):
        outs[0][:, 2 * h * V_HEAD:(2 * h + 1) * V_HEAD] = acc[:, h * V_HEAD:(h + 1) * V_HEAD].astype(outs[0].dtype)
        outs[0][:, (2 * h + 1) * V_HEAD:(2 * h + 2) * V_HEAD] = jnp.ones((acc.shape[0], V_HEAD), outs[0].dtype)


def _mm_kgrid_kernel(a_ref, w_ref, r_ref, o_ref, acc_ref):
    k = pl.program_id(2)

    @pl.when(k == 0)
    def _():
        acc_ref[...] = jnp.zeros_like(acc_ref)

    acc_ref[...] += jnp.dot(a_ref[...], w_ref[...], preferred_element_type=F32)

    @pl.when(k == pl.num_programs(2) - 1)
    def _():
        o_ref[...] = (r_ref[...] + acc_ref[...]).astype(o_ref.dtype)


def _mm_kgrid_residual(a, w, res, *, tm, tn, tk, name):
    m, kdim = a.shape
    n = w.shape[1]
    assert m % tm == 0 and n % tn == 0 and kdim % tk == 0
    return pl.pallas_call(
        _mm_kgrid_kernel,
        out_shape=jax.ShapeDtypeStruct((m, n), res.dtype),
        grid=(m // tm, n // tn, kdim // tk),
        in_specs=[pl.BlockSpec((tm, tk), lambda i, j, k: (i, k)),
                  pl.BlockSpec((tk, tn), lambda i, j, k: (k, j)),
                  pl.BlockSpec((tm, tn), lambda i, j, k: (i, j))],
        out_specs=pl.BlockSpec((tm, tn), lambda i, j, k: (i, j)),
        scratch_shapes=[pltpu.VMEM((tm, tn), F32)],
        compiler_params=_params("parallel", "parallel", "arbitrary"),
        name=name,
    )(a, w, res)


def _rope_table_kernel(pos_ref, f_ref, cos_ref, slo_ref, shi_ref):
    ang = pos_ref[...] * f_ref[...]
    lane = lax.broadcasted_iota(jnp.int32, ang.shape, 1)
    half = QK_ROPE // 2
    c = jnp.cos(ang)
    s = jnp.sin(ang)
    cos_ref[...] = jnp.where(lane < QK_ROPE, c, 0.0)
    slo_ref[...] = jnp.where(lane < half, -s, 0.0)
    shi_ref[...] = jnp.where((lane >= half) & (lane < QK_ROPE), s, 0.0)


def _rope_tables(positions):
    t = positions.size
    inv_freq = ROPE_THETA ** (-jnp.arange(0, QK_ROPE, 2, dtype=F32) / QK_ROPE)
    f = jnp.concatenate([inv_freq, inv_freq, jnp.zeros((LANES - QK_ROPE,), F32)]).reshape(1, LANES)
    pos = positions.astype(F32).reshape(t, 1)
    tm = _tile(t, 1024, SUBLANES)
    shp = jax.ShapeDtypeStruct((t, LANES), F32)
    return pl.pallas_call(
        _rope_table_kernel,
        out_shape=[shp, shp, shp],
        grid=(t // tm,),
        in_specs=[pl.BlockSpec((tm, 1), lambda i: (i, 0)), pl.BlockSpec((1, LANES), lambda i: (0, 0))],
        out_specs=[pl.BlockSpec((tm, LANES), lambda i: (i, 0))] * 3,
        compiler_params=_params("parallel"),
        name="rope_tables",
    )(pos, f)


def _sigmoid(x):
    return 0.5 * jnp.tanh(0.5 * x) + 0.5


def _log_sigmoid(x):
    return jnp.minimum(x, 0.0) - jnp.log1p(jnp.exp(-jnp.abs(x)))


def _lru_kernel(*refs, reverse, combine, ts, nc, seq):
    (xa_ref, prev_ref, next_ref, cw_ref, cb_ref, wa_ref, ba_ref, wx_ref, bx_ref, lam_ref) = refs[:10]
    if combine:
        hf_ref, gg_ref = refs[10:12]
        refs = refs[12:]
    else:
        refs = refs[10:]
    o_ref, buf, a_s, b_s, h_s, p_s, carry = refs
    bw = xa_ref.shape[-1]
    nl = bw // LANES
    seg = ts // SUBLANES
    pitch = seg + SCAN_PITCH_PAD
    halo = SUBLANES

    c = pl.program_id(2)
    cc = (nc - 1 - c) if reverse else c

    @pl.when(c == 0)
    def _():
        carry[...] = jnp.zeros_like(carry)

    buf[0:halo, :] = jnp.where(cc > 0, prev_ref[...], 0.0)
    buf[halo:halo + ts, :] = xa_ref[...]
    buf[halo + ts:2 * halo + ts, :] = jnp.where(cc < nc - 1, next_ref[...], 0.0)
    left = CONV_WIDTH // 2
    xc = cb_ref[...]
    for k in range(CONV_WIDTH):
        xc = xc + buf[halo - left + k:halo - left + k + ts, :] * cw_ref[k:k + 1, :]

    xb = xc.astype(BF16)
    r = _sigmoid(jnp.dot(xb, wa_ref[0], preferred_element_type=F32) + ba_ref[...])
    i = _sigmoid(jnp.dot(xb, wx_ref[0], preferred_element_type=F32) + bx_ref[...])
    log_a = (LRU_C * r) * _log_sigmoid(lam_ref[...])
    a = jnp.exp(log_a)
    m2 = -jnp.tanh(log_a) * (1.0 + a * a)
    mult = jnp.where(m2 > 0.0, m2 * lax.rsqrt(m2), 0.0)
    row = lax.broadcasted_iota(jnp.int32, (ts, 1), 0) + cc * ts
    mult = jnp.where(row == (seq - 1 if reverse else 0), 1.0, mult)
    b = mult * (i * xc)

    for j in range(SUBLANES):
        for l in range(nl):
            a_s[l, j * pitch:j * pitch + seg, :] = a[j * seg:(j + 1) * seg, l * LANES:(l + 1) * LANES]
            b_s[l, j * pitch:j * pitch + seg, :] = b[j * seg:(j + 1) * seg, l * LANES:(l + 1) * LANES]
    hloc = [jnp.zeros((SUBLANES, LANES), F32) for _ in range(nl)]
    prod = [jnp.ones((SUBLANES, LANES), F32) for _ in range(nl)]
    for t in (range(seg - 1, -1, -1) if reverse else range(seg)):
        for l in range(nl):
            at = a_s[l, pl.ds(t, SUBLANES, stride=pitch), :]
            bt = b_s[l, pl.ds(t, SUBLANES, stride=pitch), :]
            hloc[l] = at * hloc[l] + bt
            prod[l] = at * prod[l]
            h_s[l, pl.ds(t, SUBLANES, stride=pitch), :] = hloc[l]
            p_s[l, pl.ds(t, SUBLANES, stride=pitch), :] = prod[l]
    cj = [carry[:, l * LANES:(l + 1) * LANES] for l in range(nl)]
    for j in (range(SUBLANES - 1, -1, -1) if reverse else range(SUBLANES)):
        rows = slice(j * seg, (j + 1) * seg)
        for l in range(nl):
            lanes = slice(l * LANES, (l + 1) * LANES)
            h = h_s[l, j * pitch:j * pitch + seg, :] + p_s[l, j * pitch:j * pitch + seg, :] * cj[l]
            if combine:
                h = gg_ref[rows, lanes] * (hf_ref[rows, lanes] + h)
            o_ref[rows, lanes] = h.astype(o_ref.dtype)
            cj[l] = hloc[l][j:j + 1, :] + prod[l][j:j + 1, :] * cj[l]
    for l in range(nl):
        carry[:, l * LANES:(l + 1) * LANES] = cj[l]


def _lru(xa, conv_w, conv_b, wa, ba, wx, bx, lam, *, reverse, hf=None, gg=None, out_dtype, name):
    bsz, seq, w = xa.shape
    heads, bw, _ = wa.shape
    ts = _tile(seq, 512, SUBLANES * SUBLANES)
    nc = seq // ts
    nhalo = ts // SUBLANES
    combine = hf is not None

    def cidx(c):
        return (nc - 1 - c) if reverse else c

    blk = lambda b, h, c: (b, cidx(c), h)
    in_specs = [
        pl.BlockSpec((None, ts, bw), blk),
        pl.BlockSpec((None, SUBLANES, bw), lambda b, h, c: (b, jnp.maximum(cidx(c) * nhalo - 1, 0), h)),
        pl.BlockSpec((None, SUBLANES, bw),
                     lambda b, h, c: (b, jnp.minimum((cidx(c) + 1) * nhalo, seq // SUBLANES - 1), h)),
        pl.BlockSpec((CONV_WIDTH, bw), lambda b, h, c: (0, h)),
        pl.BlockSpec((1, bw), lambda b, h, c: (0, h)),
        pl.BlockSpec((1, bw, bw), lambda b, h, c: (h, 0, 0)),
        pl.BlockSpec((1, bw), lambda b, h, c: (0, h)),
        pl.BlockSpec((1, bw, bw), lambda b, h, c: (h, 0, 0)),
        pl.BlockSpec((1, bw), lambda b, h, c: (0, h)),
        pl.BlockSpec((1, bw), lambda b, h, c: (0, h)),
    ]
    args = [xa, xa, xa, conv_w.astype(F32), conv_b.reshape(1, w).astype(F32),
            wa.astype(BF16), ba.reshape(1, w).astype(F32), wx.astype(BF16), bx.reshape(1, w).astype(F32),
            lam.reshape(1, w).astype(F32)]
    if combine:
        in_specs += [pl.BlockSpec((None, ts, bw), blk), pl.BlockSpec((None, ts, bw), blk)]
        args += [hf, gg]
    nl = bw // LANES
    rows = SUBLANES * (ts // SUBLANES + SCAN_PITCH_PAD)
    scan_buf = pltpu.VMEM((nl, rows, LANES), F32)
    kern = functools.partial(_lru_kernel, reverse=reverse, combine=combine, ts=ts, nc=nc, seq=seq)
    return pl.pallas_call(
        kern,
        out_shape=jax.ShapeDtypeStruct((bsz, seq, w), out_dtype),
        grid=(bsz, heads, nc),
        in_specs=in_specs,
        out_specs=pl.BlockSpec((None, ts, bw), blk),
        scratch_shapes=[pltpu.VMEM((ts + 2 * SUBLANES, bw), F32), scan_buf, scan_buf, scan_buf, scan_buf,
                        pltpu.VMEM((1, bw), F32)],
        compiler_params=_params("parallel", "parallel", "arbitrary"),
        name=name,
    )(*args)


def _mla_kernel(q_ref, k_ref, v_ref, o_ref, m_s, acc_s, s0_s, s1_s, *, tq, tk):
    seq = k_ref.shape[0]
    nq, nk = seq // tq, seq // tk
    reps = tk // LANES
    s_bufs = (s0_s, s1_s)

    def scores(qi, kj):
        qoff = pl.multiple_of(qi * tq, tq)
        return lax.dot_general(q_ref[pl.ds(qoff, tq), :], k_ref[kj * tk:(kj + 1) * tk, :],
                               (((1,), (1,)), ((), ())), preferred_element_type=F32)

    def consume(s_ref, kj):
        s = s_ref[...]
        m_blk = jnp.max(s, axis=-1, keepdims=True)
        m_new = jnp.broadcast_to(m_blk, (tq, LANES)) if kj == 0 else jnp.maximum(m_s[...], m_blk)
        p = jnp.exp2(s - jnp.concatenate([m_new] * reps, axis=1))
        pv = jnp.dot(p.astype(BF16), v_ref[kj * tk:(kj + 1) * tk, :], preferred_element_type=F32)
        if kj == 0:
            acc_s[...] = pv
        else:
            alpha = jnp.exp2(m_s[...] - m_new)
            acc_s[...] = jnp.concatenate([alpha, alpha], axis=1) * acc_s[...] + pv
        m_s[...] = m_new

    s0_s[...] = scores(0, 0)

    def body(qi, _):
        for kj in range(nk):
            nxt = s_bufs[(kj + 1) % 2]
            if kj + 1 < nk:
                nxt[...] = scores(qi, kj + 1)
            else:
                nxt[...] = scores(jnp.minimum(qi + 1, nq - 1), 0)
            consume(s_bufs[kj % 2], kj)
        qoff = pl.multiple_of(qi * tq, tq)
        o_ref[pl.ds(qoff, tq), :] = (acc_s[:, :V_HEAD] / acc_s[:, V_HEAD:]).astype(o_ref.dtype)
        return 0

    lax.fori_loop(0, nq, body, 0)


def _mla_attention(q, k, v, bsz, seq):
    heads = q.shape[1] // QK_PAD
    tq = _tile(seq, 512, SUBLANES)
    tk = _tile(seq // 2, 1024, LANES)
    assert (seq // tk) % 2 == 0
    kern = functools.partial(_mla_kernel, tq=tq, tk=tk)
    return pl.pallas_call(
        kern,
        out_shape=jax.ShapeDtypeStruct((bsz * seq, heads * V_HEAD), BF16),
        grid=(bsz, heads),
        in_specs=[pl.BlockSpec((seq, QK_PAD), lambda b, h: (b, h)),
                  pl.BlockSpec((seq, QK_PAD), lambda b, h: (b, h)),
                  pl.BlockSpec((seq, 2 * V_HEAD), lambda b, h: (b, h))],
        out_specs=pl.BlockSpec((seq, V_HEAD), lambda b, h: (b, h)),
        scratch_shapes=[pltpu.VMEM((tq, LANES), F32), pltpu.VMEM((tq, 2 * V_HEAD), F32),
                        pltpu.VMEM((tq, tk), F32), pltpu.VMEM((tq, tk), F32)],
        compiler_params=_params("parallel", "arbitrary"),
        name="mla_attention",
    )(q, k, v)


def _mem_attn_kernel(q_ref, kv_ref, o_ref, *, heads, hd):
    for h in range(heads):
        q = q_ref[:, h * hd:(h + 1) * hd]
        k = kv_ref[:, 2 * h * hd:(2 * h + 1) * hd]
        v = kv_ref[:, (2 * h + 1) * hd:(2 * h + 2) * hd]
        s = lax.dot_general(q, k, (((1,), (1,)), ((), ())), preferred_element_type=F32)
        p = jnp.exp(s - jnp.max(s, axis=-1, keepdims=True))
        l = jnp.sum(p, axis=-1, keepdims=True)
        o = jnp.dot(p.astype(BF16), v, preferred_element_type=F32) / l
        o_ref[:, h * hd:(h + 1) * hd] = o.astype(o_ref.dtype)


def _mem_attention(q, kv, bsz, seq, mem_len):
    width = q.shape[1]
    hd = width // MEM_HEADS
    tq = _tile(seq, 512, SUBLANES)
    nq = seq // tq
    kern = functools.partial(_mem_attn_kernel, heads=MEM_HEADS, hd=hd)
    return pl.pallas_call(
        kern,
        out_shape=jax.ShapeDtypeStruct((bsz * seq, width), BF16),
        grid=(bsz, nq),
        in_specs=[pl.BlockSpec((tq, width), lambda b, i: (b * nq + i, 0)),
                  pl.BlockSpec((mem_len, 2 * width), lambda b, i: (b, 0))],
        out_specs=pl.BlockSpec((tq, width), lambda b, i: (b * nq + i, 0)),
        compiler_params=_params("parallel", "arbitrary"),
        name="mem_attention",
    )(q, kv)


def _layer(x, mem, tables, bsz, seq, p):
    t, d = x.shape
    w_in = p["w_in"]
    lru_w = p["conv_w"].shape[1]
    q_lora = p["q_a_norm"].shape[0]
    kv_lora = p["kv_a_norm"].shape[0]
    cos, sin_lo, sin_hi = tables
    tm = _tile(t, 1024, SUBLANES)

    o_xa, o_ga, o_cq, o_ckv, o_kr = 0, lru_w, 2 * lru_w, 2 * lru_w + q_lora, 2 * lru_w + q_lora + kv_lora
    o_gate = o_kr + QK_ROPE
    w_xa = w_in[:, o_xa:o_ga].astype(BF16)
    w_ga = w_in[:, o_ga:o_cq].astype(BF16)
    w_cq = w_in[:, o_cq:o_ckv].astype(BF16)
    w_ckv = jnp.pad(w_in[:, o_ckv:o_gate], ((0, 0), (0, LANES - QK_ROPE))).astype(BF16)
    w_gates = w_in[:, o_gate:].astype(BF16)

    h1 = _rmsnorm(x, p["norm_mix"], BF16, "norm_mix")
    tn = _tile(lru_w, 512, LANES)
    (xa,) = _mm([h1], [w_xa], [0], [(tn, F32)], _ep_plain, tm=tm, tn=tn, name="proj_xa")
    (gg,) = _mm([h1], [w_ga], [0], [(tn, F32)], _ep_gelu, tm=tm, tn=tn, name="proj_ga")
    tn = _tile(2 * d, 512, LANES)
    proj_names = ("w_proj_lru", "w_proj_mla", "w_out")
    proj_casts = []
    if all(_cast_rows_per_step(p[nm].shape[0], p[nm].shape[0], (t // tm) * (2 * d // tn)) for nm in proj_names):
        proj_casts = [(p[nm], p[nm].shape[0], p[nm].shape[1]) for nm in proj_names]
    gates, *proj_w = _mm([h1], [w_gates], [0], [(tn, F32)], _ep_sigmoid, tm=tm, tn=tn, casts=proj_casts,
                         name="proj_gates")
    if not proj_w:
        proj_w = [p[nm].astype(BF16) for nm in proj_names]
    w_proj_lru, w_proj_mla, w_out = proj_w
    (cqn,) = _mm([h1], [w_cq], [0], [(q_lora, BF16)], _ep_rms, tm=tm, tn=q_lora,
                 col_ops=[p["q_a_norm"].reshape(1, q_lora).astype(F32)], name="proj_cq")
    g_kv = jnp.pad(p["kv_a_norm"].astype(F32), (0, LANES)).reshape(1, kv_lora + LANES)
    ckvn, k_rope = _mm([h1], [w_ckv], [0], [(kv_lora, BF16), (LANES, BF16)],
                       functools.partial(_ep_ckv, kv_lora), tm=tm, tn=kv_lora + LANES,
                       row_ops=[cos, sin_lo, sin_hi], col_ops=[g_kv], name="proj_ckv")

    xa3 = xa.reshape(bsz, seq, lru_w)
    h_f = _lru(xa3, p["conv_w"], p["conv_b"], p["lru_wa_f"], p["lru_ba_f"], p["lru_wx_f"], p["lru_bx_f"],
               p["lru_lam_f"], reverse=False, out_dtype=F32, name="lru_fwd")
    y_a = _lru(xa3, p["conv_w"], p["conv_b"], p["lru_wa_b"], p["lru_ba_b"], p["lru_wx_b"], p["lru_bx_b"],
               p["lru_lam_b"], reverse=True, hf=h_f, gg=gg.reshape(bsz, seq, lru_w), out_dtype=BF16,
               name="lru_bwd").reshape(t, lru_w)

    heads = MLA_HEADS
    scale = float(QK_NOPE + QK_ROPE) ** -0.5 * LOG2_E
    w_uq = jnp.pad(p["w_uq"].reshape(q_lora, heads, QK_NOPE + QK_ROPE),
                   ((0, 0), (0, 0), (0, QK_PAD - QK_NOPE - QK_ROPE))).reshape(q_lora, heads * QK_PAD).astype(BF16)
    w_ukv = p["w_ukv"].reshape(kv_lora, heads, QK_NOPE + V_HEAD)
    w_uk = w_ukv[:, :, :QK_NOPE].reshape(kv_lora, heads * QK_NOPE).astype(BF16)
    w_uv = w_ukv[:, :, QK_NOPE:].reshape(kv_lora, heads * V_HEAD).astype(BF16)
    hpt = min(4, heads)
    (q,) = _mm([cqn], [w_uq], [0], [(hpt * QK_PAD, BF16)], functools.partial(_ep_q, scale), tm=tm,
               tn=hpt * QK_PAD, row_ops=[cos, sin_lo, sin_hi], name="proj_q")
    (k,) = _mm([ckvn], [w_uk], [0], [(hpt * QK_PAD, BF16)], _ep_k, tm=tm, tn=hpt * QK_NOPE,
               row_ops=[k_rope], name="proj_k")
    (v,) = _mm([ckvn], [w_uv], [0], [(2 * hpt * V_HEAD, BF16)], _ep_v, tm=tm, tn=hpt * V_HEAD, name="proj_v")
    y_b = _mla_attention(q, k, v, bsz, seq)

    d_ff = p["w_ffn_gate"].shape[1]
    tnf = 512 if d_ff >= 512 else LANES
    d_ffp = -(-d_ff // (2 * tnf)) * (2 * tnf)

    tn = _tile(d, 512, LANES)
    tm2 = _tile(t, 512, SUBLANES)
    up_casts = []
    if _cast_rows_per_step(d, d, (t // tm2) * (d // tn)):
        up_casts = [(p["w_ffn_gate"], d, d_ffp), (p["w_ffn_up"], d, d_ffp)]
    merged, *up_w = _mm([y_a, y_b], [w_proj_lru, w_proj_mla], [0, 1],
                        [(tn, BF16)], _ep_merge, tm=tm2, tn=tn, tile_ops=[(gates, 0), (gates, d // tn)],
                        casts=up_casts, name="proj_merge")
    if not up_w:
        up_w = [jnp.pad(p[nm].astype(BF16), ((0, 0), (0, d_ffp - d_ff))) for nm in ("w_ffn_gate", "w_ffn_up")]
    w_g, w_u = up_w
    (x1,) = _mm([merged], [w_out], [0], [(tn, F32)], _ep_residual, tm=tm, tn=tn,
                tile_ops=[(x, 0)], name="proj_out")

    mem_w = p["w_mem_q"].shape[1]
    mem_len = mem.shape[0] // bsz
    hq = _rmsnorm(x1, p["norm_mem_x"], BF16, "norm_mem_x")
    hm = _rmsnorm(mem, p["norm_mem_kv"], BF16, "norm_mem_kv")
    tnm = _tile(mem_w, 512, LANES)
    (mq,) = _mm([hq], [p["w_mem_q"].astype(BF16)], [0], [(tnm, BF16)],
                functools.partial(_ep_scale, float(mem_w // MEM_HEADS) ** -0.5), tm=tm, tn=tnm, name="mem_q")
    (mkv,) = _mm([hm], [p["w_mem_kv"].astype(BF16)], [0], [(tnm, BF16)], _ep_plain,
                 tm=_tile(mem.shape[0], 512, SUBLANES), tn=tnm, name="mem_kv")
    mo = _mem_attention(mq, mkv, bsz, seq, mem_len)
    (x2,) = _mm([mo], [p["w_mem_o"].astype(BF16)], [0], [(tn, F32)], _ep_residual, tm=tm, tn=tn,
                tile_ops=[(x1, 0)], name="mem_o")

    hf = _rmsnorm(x2, p["norm_ffn"], BF16, "norm_ffn")
    down_casts = []
    if _cast_rows_per_step(d_ff, d_ffp, (t // tm) * (d_ffp // tnf)):
        down_casts = [(p["w_ffn_down"], d_ffp, d)]
    act, *down_w = _mm([hf], [w_g, w_u], [0, 0], [(tnf, BF16)], _ep_swiglu, tm=tm, tn=tnf,
                       casts=down_casts, name="ffn_up")
    w_d = down_w[0] if down_w else jnp.pad(p["w_ffn_down"].astype(BF16), ((0, d_ffp - d_ff), (0, 0)))
    x3 = _mm_kgrid_residual(act, w_d, x2, tm=tm, tn=_tile(d, 1024, LANES), tk=d_ffp // 4 if d_ffp >= 2048 else d_ffp,
                            name="ffn_down")
    return x3


def kernel(x, mem, positions, norm_mix, w_in, conv_w, conv_b, lru_wa_f, lru_ba_f, lru_wx_f, lru_bx_f, lru_lam_f,
           lru_wa_b, lru_ba_b, lru_wx_b, lru_bx_b, lru_lam_b, q_a_norm, w_uq, kv_a_norm, w_ukv, w_proj_lru,
           w_proj_mla, w_out, norm_mem_x, norm_mem_kv, w_mem_q, w_mem_kv, w_mem_o, norm_ffn, w_ffn_gate,
           w_ffn_up, w_ffn_down, norm_final):
    bsz, seq, d = x.shape
    stacked = dict(norm_mix=norm_mix, w_in=w_in, conv_w=conv_w, conv_b=conv_b, lru_wa_f=lru_wa_f,
                   lru_ba_f=lru_ba_f, lru_wx_f=lru_wx_f, lru_bx_f=lru_bx_f, lru_lam_f=lru_lam_f,
                   lru_wa_b=lru_wa_b, lru_ba_b=lru_ba_b, lru_wx_b=lru_wx_b, lru_bx_b=lru_bx_b,
                   lru_lam_b=lru_lam_b, q_a_norm=q_a_norm, w_uq=w_uq, kv_a_norm=kv_a_norm, w_ukv=w_ukv,
                   w_proj_lru=w_proj_lru, w_proj_mla=w_proj_mla, w_out=w_out, norm_mem_x=norm_mem_x,
                   norm_mem_kv=norm_mem_kv, w_mem_q=w_mem_q, w_mem_kv=w_mem_kv, w_mem_o=w_mem_o,
                   norm_ffn=norm_ffn, w_ffn_gate=w_ffn_gate, w_ffn_up=w_ffn_up, w_ffn_down=w_ffn_down)
    tables = _rope_tables(positions)
    xf = x.reshape(bsz * seq, d)
    memf = mem.reshape(-1, d)
    for layer in range(norm_mix.shape[0]):
        xf = _layer(xf, memf, tables, bsz, seq, {k: v[layer] for k, v in stacked.items()})
    out = _rmsnorm(xf, norm_final, x.dtype, "norm_final")
    return out.reshape(bsz, seq, d)
```

```python
import functools

import jax
import jax.numpy as jnp
from jax import lax
from jax.experimental import pallas as pl
from jax.experimental.pallas import tpu as pltpu

F32 = jnp.float32
BF16 = jnp.bfloat16

EPS = 1e-6
CONV_WIDTH = 4
LRU_C = 8.0
MLA_HEADS = 32
QK_NOPE = 128
QK_ROPE = 64
V_HEAD = 128
ROPE_THETA = 10000.0
MEM_HEADS = 4
LOG2_E = 1.4426950408889634

LANES = 128
SUBLANES = 8
VMEM_LIMIT_BYTES = 56 * 1024 * 1024

QK_PAD = 2 * LANES
SCAN_PITCH_PAD = 4


def _tile(n, pref, align):
    if n <= pref:
        return n
    t = (pref // align) * align
    while t > align and n % t:
        t -= align
    assert n % t == 0, (n, pref, align)
    return t


def _params(*sem, flags=None):
    return pltpu.CompilerParams(dimension_semantics=sem, vmem_limit_bytes=VMEM_LIMIT_BYTES, flags=flags)


def _rms(x, g):
    inv = lax.rsqrt(jnp.mean(x * x, axis=-1, keepdims=True) + EPS)
    return x * inv * g


def _rmsnorm_kernel(x_ref, g_ref, o_ref):
    o_ref[...] = _rms(x_ref[...].astype(F32), g_ref[...]).astype(o_ref.dtype)


def _rmsnorm(x, g, out_dtype, name):
    m, d = x.shape
    tm = _tile(m, 256, SUBLANES)
    return pl.pallas_call(
        _rmsnorm_kernel,
        out_shape=jax.ShapeDtypeStruct((m, d), out_dtype),
        grid=(m // tm,),
        in_specs=[pl.BlockSpec((tm, d), lambda i: (i, 0)), pl.BlockSpec((1, d), lambda i: (0, 0))],
        out_specs=pl.BlockSpec((tm, d), lambda i: (i, 0)),
        compiler_params=_params("parallel"),
        name=name,
    )(x, g.reshape(1, d).astype(F32))


def _mm_kernel(*refs, n_a, pairs, n_tile, n_row, n_col, n_out, casts, epilogue):
    a_refs = refs[:n_a]
    w_refs = refs[n_a:n_a + len(pairs)]
    p = n_a + len(pairs)
    tile_refs = refs[p:p + n_tile]
    row_refs = refs[p + n_tile:p + n_tile + n_row]
    col_refs = refs[p + n_tile + n_row:p + n_tile + n_row + n_col]
    p += n_tile + n_row + n_col
    cast_in = refs[p:p + len(casts)]
    o_refs = refs[p + len(casts):p + len(casts) + n_out]
    cast_out = refs[p + len(casts) + n_out:]
    accs = [jnp.dot(a_refs[ai][...], w_ref[...], preferred_element_type=F32)
            for ai, w_ref in zip(pairs, w_refs)]
    epilogue(accs, tile_refs, row_refs, col_refs, o_refs)
    step = pl.program_id(0) * pl.num_programs(1) + pl.program_id(1)
    for src, dst, valid_blocks in zip(cast_in, cast_out, casts):
        cols = src.shape[1]
        dst[:, :cols] = jnp.where(step < valid_blocks, src[...], 0.0).astype(dst.dtype)
        if dst.shape[1] > cols:
            dst[:, cols:] = jnp.zeros((dst.shape[0], dst.shape[1] - cols), dst.dtype)


def _cast_rows_per_step(src_rows, out_rows, steps):
    rps = out_rows // steps
    ok = rps * steps == out_rows and rps % (2 * SUBLANES) == 0 and src_rows % rps == 0
    return rps if ok else None


def _mm(a_ops, w_ops, pairs, outs, epilogue, *, tm, tn, tile_ops=(), row_ops=(), col_ops=(), casts=(), name):
    m = a_ops[0].shape[0]
    n = w_ops[0].shape[1]
    assert m % tm == 0 and n % tn == 0, (m, n, tm, tn)
    nj = n // tn
    cast_specs_in, cast_specs_out, cast_shapes, cast_valid = [], [], [], []
    for src, out_rows, out_cols in casts:
        rps = _cast_rows_per_step(src.shape[0], out_rows, (m // tm) * nj)
        valid = src.shape[0] // rps
        cast_specs_in.append(pl.BlockSpec(
            (rps, src.shape[1]), functools.partial(lambda v, i, j: (jnp.minimum(i * nj + j, v - 1), 0), valid)))
        cast_specs_out.append(pl.BlockSpec((rps, out_cols), lambda i, j: (i * nj + j, 0)))
        cast_shapes.append(jax.ShapeDtypeStruct((out_rows, out_cols), BF16))
        cast_valid.append(valid)
    in_specs = [pl.BlockSpec((tm, a.shape[1]), lambda i, j: (i, 0)) for a in a_ops]
    in_specs += [pl.BlockSpec((w.shape[0], tn), lambda i, j: (0, j)) for w in w_ops]
    in_specs += [pl.BlockSpec((tm, tn), functools.partial(lambda off, i, j: (i, j + off), off))
                 for _, off in tile_ops]
    in_specs += [pl.BlockSpec((tm, r.shape[1]), lambda i, j: (i, 0)) for r in row_ops]
    in_specs += [pl.BlockSpec((1, tn), lambda i, j: (0, j)) for _ in col_ops]
    in_specs += cast_specs_in
    out_shape = [jax.ShapeDtypeStruct((m, (n // tn) * w), dt) for w, dt in outs] + cast_shapes
    out_specs = [pl.BlockSpec((tm, w), lambda i, j: (i, j)) for w, _ in outs] + cast_specs_out
    kern = functools.partial(_mm_kernel, n_a=len(a_ops), pairs=tuple(pairs), n_tile=len(tile_ops),
                             n_row=len(row_ops), n_col=len(col_ops), n_out=len(outs),
                             casts=tuple(cast_valid), epilogue=epilogue)
    res = pl.pallas_call(
        kern,
        out_shape=out_shape,
        grid=(m // tm, n // tn),
        in_specs=in_specs,
        out_specs=out_specs,
        compiler_params=_params("parallel", "arbitrary"),
        name=name,
    )(*a_ops, *w_ops, *[t for t, _ in tile_ops], *row_ops, *col_ops, *[c[0] for c in casts])
    return res


def _ep_plain(accs, tiles, rows, cols, outs):
    outs[0][...] = accs[0].astype(outs[0].dtype)


def _ep_gelu(accs, tiles, rows, cols, outs):
    outs[0][...] = jax.nn.gelu(accs[0]).astype(outs[0].dtype)


def _ep_sigmoid(accs, tiles, rows, cols, outs):
    outs[0][...] = jax.nn.sigmoid(accs[0]).astype(outs[0].dtype)


def _ep_rms(accs, tiles, rows, cols, outs):
    outs[0][...] = _rms(accs[0], cols[0][...]).astype(outs[0].dtype)


def _ep_scale(scale, accs, tiles, rows, cols, outs):
    outs[0][...] = (accs[0] * scale).astype(outs[0].dtype)


def _ep_residual(accs, tiles, rows, cols, outs):
    outs[0][...] = (tiles[0][...] + accs[0]).astype(outs[0].dtype)


def _ep_merge(accs, tiles, rows, cols, outs):
    outs[0][...] = (tiles[0][...] * accs[0] + tiles[1][...] * accs[1]).astype(outs[0].dtype)


def _ep_swiglu(accs, tiles, rows, cols, outs):
    outs[0][...] = (jax.nn.silu(accs[0]) * accs[1]).astype(outs[0].dtype)


def _rope(x, cos, sin_lo, sin_hi):
    half = QK_ROPE // 2
    return x * cos + pltpu.roll(x, half, 1) * sin_hi + pltpu.roll(x, LANES - half, 1) * sin_lo


def _ep_ckv(kv_lora, accs, tiles, rows, cols, outs):
    acc = accs[0]
    outs[0][...] = _rms(acc[:, :kv_lora], cols[0][:, :kv_lora]).astype(outs[0].dtype)
    kr = _rope(acc[:, kv_lora:kv_lora + LANES], rows[0][...], rows[1][...], rows[2][...])
    outs[1][...] = kr.astype(outs[1].dtype)


def _ep_q(scale, accs, tiles, rows, cols, outs):
    acc = accs[0]
    cos, sin_lo, sin_hi = rows[0][...], rows[1][...], rows[2][...]
    for h in range(acc.shape[1] // QK_PAD):
        c0 = h * QK_PAD
        outs[0][:, c0:c0 + QK_NOPE] = (acc[:, c0:c0 + QK_NOPE] * scale).astype(outs[0].dtype)
        rp = _rope(acc[:, c0 + QK_NOPE:c0 + QK_PAD], cos, sin_lo, sin_hi) * scale
        outs[0][:, c0 + QK_NOPE:c0 + QK_PAD] = rp.astype(outs[0].dtype)


def _ep_k(accs, tiles, rows, cols, outs):
    acc = accs[0]
    kr = rows[0][...]
    for h in range(acc.shape[1] // QK_NOPE):
        outs[0][:, h * QK_PAD:h * QK_PAD + QK_NOPE] = acc[:, h * QK_NOPE:(h + 1) * QK_NOPE].astype(outs[0].dtype)
        outs[0][:, h * QK_PAD + QK_NOPE:(h + 1) * QK_PAD] = kr


def _ep_v(accs, tiles, rows, cols, outs):
    acc = accs[0]
    for h in range(acc.shape[1] // V_HEAD):
        outs[0][:, 2 * h * V_HEAD:(2 * h + 1) * V_HEAD] = acc[:, h * V_HEAD:(h + 1) * V_HEAD].astype(outs[0].dtype)
        outs[0][:, (2 * h + 1) * V_HEAD:(2 * h + 2) * V_HEAD] = jnp.ones((acc.shape[0], V_HEAD), outs[0].dtype)


def _mm_kgrid_kernel(a_ref, w_ref, r_ref, o_ref, acc_ref):
    k = pl.program_id(2)

    @pl.when(k == 0)
    def _():
        acc_ref[...] = jnp.zeros_like(acc_ref)

    acc_ref[...] += jnp.dot(a_ref[...], w_ref[...], preferred_element_type=F32)

    @pl.when(k == pl.num_programs(2) - 1)
    def _():
        o_ref[...] = (r_ref[...] + acc_ref[...]).astype(o_ref.dtype)


def _mm_kgrid_residual(a, w, res, *, tm, tn, tk, name):
    m, kdim = a.shape
    n = w.shape[1]
    assert m % tm == 0 and n % tn == 0 and kdim % tk == 0
    return pl.pallas_call(
        _mm_kgrid_kernel,
        out_shape=jax.ShapeDtypeStruct((m, n), res.dtype),
        grid=(m // tm, n // tn, kdim // tk),
        in_specs=[pl.BlockSpec((tm, tk), lambda i, j, k: (i, k)),
                  pl.BlockSpec((tk, tn), lambda i, j, k: (k, j)),
                  pl.BlockSpec((tm, tn), lambda i, j, k: (i, j))],
        out_specs=pl.BlockSpec((tm, tn), lambda i, j, k: (i, j)),
        scratch_shapes=[pltpu.VMEM((tm, tn), F32)],
        compiler_params=_params("parallel", "parallel", "arbitrary"),
        name=name,
    )(a, w, res)


def _rope_table_kernel(pos_ref, f_ref, cos_ref, slo_ref, shi_ref):
    ang = pos_ref[...] * f_ref[...]
    lane = lax.broadcasted_iota(jnp.int32, ang.shape, 1)
    half = QK_ROPE // 2
    c = jnp.cos(ang)
    s = jnp.sin(ang)
    cos_ref[...] = jnp.where(lane < QK_ROPE, c, 0.0)
    slo_ref[...] = jnp.where(lane < half, -s, 0.0)
    shi_ref[...] = jnp.where((lane >= half) & (lane < QK_ROPE), s, 0.0)


def _rope_tables(positions):
    t = positions.size
    inv_freq = ROPE_THETA ** (-jnp.arange(0, QK_ROPE, 2, dtype=F32) / QK_ROPE)
    f = jnp.concatenate([inv_freq, inv_freq, jnp.zeros((LANES - QK_ROPE,), F32)]).reshape(1, LANES)
    pos = positions.astype(F32).reshape(t, 1)
    tm = _tile(t, 1024, SUBLANES)
    shp = jax.ShapeDtypeStruct((t, LANES), F32)
    return pl.pallas_call(
        _rope_table_kernel,
        out_shape=[shp, shp, shp],
        grid=(t // tm,),
        in_specs=[pl.BlockSpec((tm, 1), lambda i: (i, 0)), pl.BlockSpec((1, LANES), lambda i: (0, 0))],
        out_specs=[pl.BlockSpec((tm, LANES), lambda i: (i, 0))] * 3,
        compiler_params=_params("parallel"),
        name="rope_tables",
    )(pos, f)


def _sigmoid(x):
    return 0.5 * jnp.tanh(0.5 * x) + 0.5


def _log_sigmoid(x):
    return jnp.minimum(x, 0.0) - jnp.log1p(jnp.exp(-jnp.abs(x)))


def _conv_centred(x, prev, nxt, cw_ref, cb_ref):
    ts, bw = x.shape
    row8 = lax.broadcasted_iota(jnp.int32, (SUBLANES, bw), 0)

    def shifted(d):
        if d == 0:
            return x
        if d < 0:
            r = pltpu.roll(x, -d, 0)
            head = jnp.where(row8 < -d, pltpu.roll(prev, -d, 0), r[:SUBLANES])
            return jnp.concatenate([head, r[SUBLANES:]], axis=0)
        r = pltpu.roll(x, ts - d, 0)
        tail = jnp.where(row8 >= SUBLANES - d, pltpu.roll(nxt, SUBLANES - d, 0), r[ts - SUBLANES:])
        return jnp.concatenate([r[:ts - SUBLANES], tail], axis=0)

    left = CONV_WIDTH // 2
    xc = cb_ref[...]
    for k in range(CONV_WIDTH):
        xc = xc + shifted(k - left) * cw_ref[k:k + 1, :]
    return xc


def _lru_kernel(*refs, reverse, ts, nc, seq):
    if reverse:
        xc_ref, wa_ref, ba_ref, wx_ref, bx_ref, lam_ref, hf_ref, gg_ref, o_ref = refs[:9]
        refs = refs[9:]
    else:
        (xa_ref, prev_ref, next_ref, cw_ref, cb_ref, wa_ref, ba_ref, wx_ref, bx_ref, lam_ref,
         o_ref, xc_ref) = refs[:12]
        refs = refs[12:]
    a_s, b_s, h_s, p_s, carry = refs
    bw = xc_ref.shape[-1]
    nl = bw // LANES
    seg = ts // SUBLANES
    pitch = seg + SCAN_PITCH_PAD

    c = pl.program_id(2)
    cc = (nc - 1 - c) if reverse else c

    @pl.when(c == 0)
    def _():
        carry[...] = jnp.zeros_like(carry)

    if reverse:
        xc = xc_ref[...]
    else:
        prev = jnp.where(cc > 0, prev_ref[...], 0.0)
        nxt = jnp.where(cc < nc - 1, next_ref[...], 0.0)
        xc = _conv_centred(xa_ref[...], prev, nxt, cw_ref, cb_ref)
        xc_ref[...] = xc

    xb = xc.astype(BF16)
    r = _sigmoid(jnp.dot(xb, wa_ref[0], preferred_element_type=F32) + ba_ref[...])
    i = _sigmoid(jnp.dot(xb, wx_ref[0], preferred_element_type=F32) + bx_ref[...])
    log_a = r * (LRU_C * _log_sigmoid(lam_ref[...]))
    a = jnp.exp(log_a)
    m2 = -jnp.tanh(log_a) * (1.0 + a * a)
    mult = jnp.where(m2 > 0.0, m2 * lax.rsqrt(m2), 0.0)
    row = lax.broadcasted_iota(jnp.int32, (ts, 1), 0) + cc * ts
    mult = jnp.where(row == (seq - 1 if reverse else 0), 1.0, mult)
    b = mult * (i * xc)

    for j in range(SUBLANES):
        for l in range(nl):
            a_s[l, j * pitch:j * pitch + seg, :] = a[j * seg:(j + 1) * seg, l * LANES:(l + 1) * LANES]
            b_s[l, j * pitch:j * pitch + seg, :] = b[j * seg:(j + 1) * seg, l * LANES:(l + 1) * LANES]
    hloc = [jnp.zeros((SUBLANES, LANES), F32) for _ in range(nl)]
    prod = [jnp.ones((SUBLANES, LANES), F32) for _ in range(nl)]
    for t in (range(seg - 1, -1, -1) if reverse else range(seg)):
        for l in range(nl):
            at = a_s[l, pl.ds(t, SUBLANES, stride=pitch), :]
            bt = b_s[l, pl.ds(t, SUBLANES, stride=pitch), :]
            hloc[l] = at * hloc[l] + bt
            prod[l] = at * prod[l]
            h_s[l, pl.ds(t, SUBLANES, stride=pitch), :] = hloc[l]
            p_s[l, pl.ds(t, SUBLANES, stride=pitch), :] = prod[l]
    cj = [carry[:, l * LANES:(l + 1) * LANES] for l in range(nl)]
    for j in (range(SUBLANES - 1, -1, -1) if reverse else range(SUBLANES)):
        rows = slice(j * seg, (j + 1) * seg)
        for l in range(nl):
            lanes = slice(l * LANES, (l + 1) * LANES)
            h = h_s[l, j * pitch:j * pitch + seg, :] + p_s[l, j * pitch:j * pitch + seg, :] * cj[l]
            if reverse:
                h = gg_ref[rows, lanes] * (hf_ref[rows, lanes] + h)
            o_ref[rows, lanes] = h.astype(o_ref.dtype)
            cj[l] = hloc[l][j:j + 1, :] + prod[l][j:j + 1, :] * cj[l]
    for l in range(nl):
        carry[:, l * LANES:(l + 1) * LANES] = cj[l]


def _lru(x, conv, wa, ba, wx, bx, lam, *, reverse, hf=None, gg=None, out_dtype, name):
    bsz, seq, w = x.shape
    heads, bw, _ = wa.shape
    ts = _tile(seq, 512, SUBLANES * SUBLANES)
    nc = seq // ts
    nhalo = ts // SUBLANES

    def cidx(c):
        return (nc - 1 - c) if reverse else c

    blk = lambda b, h, c: (b, cidx(c), h)
    chunk = pl.BlockSpec((None, ts, bw), blk)
    row = pl.BlockSpec((1, bw), lambda b, h, c: (0, h))
    gate_w = pl.BlockSpec((1, bw, bw), lambda b, h, c: (h, 0, 0))
    gate_specs = [gate_w, row, gate_w, row, row]
    gate_args = [wa.astype(BF16), ba.reshape(1, w).astype(F32), wx.astype(BF16), bx.reshape(1, w).astype(F32),
                 lam.reshape(1, w).astype(F32)]
    if reverse:
        in_specs = [chunk] + gate_specs + [chunk, chunk]
        args = [x] + gate_args + [hf, gg]
        out_shape = jax.ShapeDtypeStruct((bsz, seq, w), out_dtype)
        out_specs = chunk
    else:
        conv_w, conv_b = conv
        in_specs = [
            chunk,
            pl.BlockSpec((None, SUBLANES, bw), lambda b, h, c: (b, jnp.maximum(c * nhalo - 1, 0), h)),
            pl.BlockSpec((None, SUBLANES, bw),
                         lambda b, h, c: (b, jnp.minimum((c + 1) * nhalo, seq // SUBLANES - 1), h)),
            pl.BlockSpec((CONV_WIDTH, bw), lambda b, h, c: (0, h)),
            row,
        ] + gate_specs
        args = [x, x, x, conv_w.astype(F32), conv_b.reshape(1, w).astype(F32)] + gate_args
        out_shape = [jax.ShapeDtypeStruct((bsz, seq, w), out_dtype), jax.ShapeDtypeStruct((bsz, seq, w), F32)]
        out_specs = [chunk, chunk]
    nl = bw // LANES
    rows = SUBLANES * (ts // SUBLANES + SCAN_PITCH_PAD)
    scan_buf = pltpu.VMEM((nl, rows, LANES), F32)
    kern = functools.partial(_lru_kernel, reverse=reverse, ts=ts, nc=nc, seq=seq)
    return pl.pallas_call(
        kern,
        out_shape=out_shape,
        grid=(bsz, heads, nc),
        in_specs=in_specs,
        out_specs=out_specs,
        scratch_shapes=[scan_buf, scan_buf, scan_buf, scan_buf, pltpu.VMEM((1, bw), F32)],
        compiler_params=_params("parallel", "parallel", "arbitrary"),
        name=name,
    )(*args)


def _mla_kernel(q_ref, k_ref, v_ref, o_ref, m_s, acc_s, s0_s, s1_s, *, tq, tk):
    seq = k_ref.shape[0]
    nq, nk = seq // tq, seq // tk
    reps = tk // LANES
    s_bufs = (s0_s, s1_s)

    def scores(qi, kj):
        qoff = pl.multiple_of(qi * tq, tq)
        return lax.dot_general(q_ref[pl.ds(qoff, tq), :], k_ref[kj * tk:(kj + 1) * tk, :],
                               (((1,), (1,)), ((), ())), preferred_element_type=F32)

    def consume(s_ref, kj):
        s = s_ref[...]
        m_blk = jnp.max(s, axis=-1, keepdims=True)
        m_new = jnp.broadcast_to(m_blk, (tq, LANES)) if kj == 0 else jnp.maximum(m_s[...], m_blk)
        p = jnp.exp2(s - jnp.concatenate([m_new] * reps, axis=1))
        pv = jnp.dot(p.astype(BF16), v_ref[kj * tk:(kj + 1) * tk, :], preferred_element_type=F32)
        if kj == 0:
            acc_s[...] = pv
        else:
            alpha = jnp.exp2(m_s[...] - m_new)
            acc_s[...] = jnp.concatenate([alpha, alpha], axis=1) * acc_s[...] + pv
        m_s[...] = m_new

    s0_s[...] = scores(0, 0)

    def body(qi, _):
        for kj in range(nk):
            nxt = s_bufs[(kj + 1) % 2]
            if kj + 1 < nk:
                nxt[...] = scores(qi, kj + 1)
            else:
                nxt[...] = scores(jnp.minimum(qi + 1, nq - 1), 0)
            consume(s_bufs[kj % 2], kj)
        qoff = pl.multiple_of(qi * tq, tq)
        o_ref[pl.ds(qoff, tq), :] = (acc_s[:, :V_HEAD] / acc_s[:, V_HEAD:]).astype(o_ref.dtype)
        return 0

    lax.fori_loop(0, nq, body, 0)


def _mla_attention(q, k, v, bsz, seq):
    heads = q.shape[1] // QK_PAD
    tq = _tile(seq, 512, SUBLANES)
    tk = _tile(seq // 2, 1024, LANES)
    assert (seq // tk) % 2 == 0
    kern = functools.partial(_mla_kernel, tq=tq, tk=tk)
    return pl.pallas_call(
        kern,
        out_shape=jax.ShapeDtypeStruct((bsz * seq, heads * V_HEAD), BF16),
        grid=(bsz, heads),
        in_specs=[pl.BlockSpec((seq, QK_PAD), lambda b, h: (b, h)),
                  pl.BlockSpec((seq, QK_PAD), lambda b, h: (b, h)),
                  pl.BlockSpec((seq, 2 * V_HEAD), lambda b, h: (b, h))],
        out_specs=pl.BlockSpec((seq, V_HEAD), lambda b, h: (b, h)),
        scratch_shapes=[pltpu.VMEM((tq, LANES), F32), pltpu.VMEM((tq, 2 * V_HEAD), F32),
                        pltpu.VMEM((tq, tk), F32), pltpu.VMEM((tq, tk), F32)],
        compiler_params=_params("parallel", "arbitrary"),
        name="mla_attention",
    )(q, k, v)


def _mem_attn_kernel(q_ref, kv_ref, o_ref, *, heads, hd):
    for h in range(heads):
        q = q_ref[:, h * hd:(h + 1) * hd]
        k = kv_ref[:, 2 * h * hd:(2 * h + 1) * hd]
        v = kv_ref[:, (2 * h + 1) * hd:(2 * h + 2) * hd]
        s = lax.dot_general(q, k, (((1,), (1,)), ((), ())), preferred_element_type=F32)
        p = jnp.exp(s - jnp.max(s, axis=-1, keepdims=True))
        l = jnp.sum(p, axis=-1, keepdims=True)
        o = jnp.dot(p.astype(BF16), v, preferred_element_type=F32) / l
        o_ref[:, h * hd:(h + 1) * hd] = o.astype(o_ref.dtype)


def _mem_attention(q, kv, bsz, seq, mem_len):
    width = q.shape[1]
    hd = width // MEM_HEADS
    tq = _tile(seq, 512, SUBLANES)
    nq = seq // tq
    kern = functools.partial(_mem_attn_kernel, heads=MEM_HEADS, hd=hd)
    return pl.pallas_call(
        kern,
        out_shape=jax.ShapeDtypeStruct((bsz * seq, width), BF16),
        grid=(bsz, nq),
        in_specs=[pl.BlockSpec((tq, width), lambda b, i: (b * nq + i, 0)),
                  pl.BlockSpec((mem_len, 2 * width), lambda b, i: (b, 0))],
        out_specs=pl.BlockSpec((tq, width), lambda b, i: (b * nq + i, 0)),
        compiler_params=_params("parallel", "arbitrary"),
        name="mem_attention",
    )(q, kv)


def _layer(x, mem, tables, bsz, seq, p):
    t, d = x.shape
    w_in = p["w_in"]
    lru_w = p["conv_w"].shape[1]
    q_lora = p["q_a_norm"].shape[0]
    kv_lora = p["kv_a_norm"].shape[0]
    cos, sin_lo, sin_hi = tables
    tm = _tile(t, 1024, SUBLANES)

    o_xa, o_ga, o_cq, o_ckv, o_kr = 0, lru_w, 2 * lru_w, 2 * lru_w + q_lora, 2 * lru_w + q_lora + kv_lora
    o_gate = o_kr + QK_ROPE
    w_xa = w_in[:, o_xa:o_ga].astype(BF16)
    w_ga = w_in[:, o_ga:o_cq].astype(BF16)
    w_cq = w_in[:, o_cq:o_ckv].astype(BF16)
    w_ckv = jnp.pad(w_in[:, o_ckv:o_gate], ((0, 0), (0, LANES - QK_ROPE))).astype(BF16)
    w_gates = w_in[:, o_gate:].astype(BF16)

    h1 = _rmsnorm(x, p["norm_mix"], BF16, "norm_mix")
    tn = _tile(lru_w, 512, LANES)
    (xa,) = _mm([h1], [w_xa], [0], [(tn, F32)], _ep_plain, tm=tm, tn=tn, name="proj_xa")
    (gg,) = _mm([h1], [w_ga], [0], [(tn, F32)], _ep_gelu, tm=tm, tn=tn, name="proj_ga")
    tn = _tile(2 * d, 512, LANES)
    proj_names = ("w_proj_lru", "w_proj_mla", "w_out")
    proj_casts = []
    if all(_cast_rows_per_step(p[nm].shape[0], p[nm].shape[0], (t // tm) * (2 * d // tn)) for nm in proj_names):
        proj_casts = [(p[nm], p[nm].shape[0], p[nm].shape[1]) for nm in proj_names]
    gates, *proj_w = _mm([h1], [w_gates], [0], [(tn, F32)], _ep_sigmoid, tm=tm, tn=tn, casts=proj_casts,
                         name="proj_gates")
    if not proj_w:
        proj_w = [p[nm].astype(BF16) for nm in proj_names]
    w_proj_lru, w_proj_mla, w_out = proj_w
    (cqn,) = _mm([h1], [w_cq], [0], [(q_lora, BF16)], _ep_rms, tm=tm, tn=q_lora,
                 col_ops=[p["q_a_norm"].reshape(1, q_lora).astype(F32)], name="proj_cq")
    g_kv = jnp.pad(p["kv_a_norm"].astype(F32), (0, LANES)).reshape(1, kv_lora + LANES)
    ckvn, k_rope = _mm([h1], [w_ckv], [0], [(kv_lora, BF16), (LANES, BF16)],
                       functools.partial(_ep_ckv, kv_lora), tm=tm, tn=kv_lora + LANES,
                       row_ops=[cos, sin_lo, sin_hi], col_ops=[g_kv], name="proj_ckv")

    xa3 = xa.reshape(bsz, seq, lru_w)
    h_f, xc = _lru(xa3, (p["conv_w"], p["conv_b"]), p["lru_wa_f"], p["lru_ba_f"], p["lru_wx_f"], p["lru_bx_f"],
                   p["lru_lam_f"], reverse=False, out_dtype=F32, name="lru_fwd")
    y_a = _lru(xc, None, p["lru_wa_b"], p["lru_ba_b"], p["lru_wx_b"], p["lru_bx_b"],
               p["lru_lam_b"], reverse=True, hf=h_f, gg=gg.reshape(bsz, seq, lru_w), out_dtype=BF16,
               name="lru_bwd").reshape(t, lru_w)

    heads = MLA_HEADS
    scale = float(QK_NOPE + QK_ROPE) ** -0.5 * LOG2_E
    w_uq = jnp.pad(p["w_uq"].reshape(q_lora, heads, QK_NOPE + QK_ROPE),
                   ((0, 0), (0, 0), (0, QK_PAD - QK_NOPE - QK_ROPE))).reshape(q_lora, heads * QK_PAD).astype(BF16)
    w_ukv = p["w_ukv"].reshape(kv_lora, heads, QK_NOPE + V_HEAD)
    w_uk = w_ukv[:, :, :QK_NOPE].reshape(kv_lora, heads * QK_NOPE).astype(BF16)
    w_uv = w_ukv[:, :, QK_NOPE:].reshape(kv_lora, heads * V_HEAD).astype(BF16)
    hpt = min(4, heads)
    (q,) = _mm([cqn], [w_uq], [0], [(hpt * QK_PAD, BF16)], functools.partial(_ep_q, scale), tm=tm,
               tn=hpt * QK_PAD, row_ops=[cos, sin_lo, sin_hi], name="proj_q")
    (k,) = _mm([ckvn], [w_uk], [0], [(hpt * QK_PAD, BF16)], _ep_k, tm=tm, tn=hpt * QK_NOPE,
               row_ops=[k_rope], name="proj_k")
    (v,) = _mm([ckvn], [w_uv], [0], [(2 * hpt * V_HEAD, BF16)], _ep_v, tm=tm, tn=hpt * V_HEAD, name="proj_v")
    y_b = _mla_attention(q, k, v, bsz, seq)

    d_ff = p["w_ffn_gate"].shape[1]
    tnf = 512 if d_ff >= 512 else LANES
    d_ffp = -(-d_ff // (2 * tnf)) * (2 * tnf)

    tn = _tile(d, 512, LANES)
    tm2 = _tile(t, 512, SUBLANES)
    up_casts = []
    if _cast_rows_per_step(d, d, (t // tm2) * (d // tn)):
        up_casts = [(p["w_ffn_gate"], d, d_ffp), (p["w_ffn_up"], d, d_ffp)]
    merged, *up_w = _mm([y_a, y_b], [w_proj_lru, w_proj_mla], [0, 1],
                        [(tn, BF16)], _ep_merge, tm=tm2, tn=tn, tile_ops=[(gates, 0), (gates, d // tn)],
                        casts=up_casts, name="proj_merge")
    if not up_w:
        up_w = [jnp.pad(p[nm].astype(BF16), ((0, 0), (0, d_ffp - d_ff))) for nm in ("w_ffn_gate", "w_ffn_up")]
    w_g, w_u = up_w
    (x1,) = _mm([merged], [w_out], [0], [(tn, F32)], _ep_residual, tm=tm, tn=tn,
                tile_ops=[(x, 0)], name="proj_out")

    mem_w = p["w_mem_q"].shape[1]
    mem_len = mem.shape[0] // bsz
    hq = _rmsnorm(x1, p["norm_mem_x"], BF16, "norm_mem_x")
    hm = _rmsnorm(mem, p["norm_mem_kv"], BF16, "norm_mem_kv")
    tnm = _tile(mem_w, 512, LANES)
    (mq,) = _mm([hq], [p["w_mem_q"].astype(BF16)], [0], [(tnm, BF16)],
                functools.partial(_ep_scale, float(mem_w // MEM_HEADS) ** -0.5), tm=tm, tn=tnm, name="mem_q")
    (mkv,) = _mm([hm], [p["w_mem_kv"].astype(BF16)], [0], [(tnm, BF16)], _ep_plain,
                 tm=_tile(mem.shape[0], 512, SUBLANES), tn=tnm, name="mem_kv")
    mo = _mem_attention(mq, mkv, bsz, seq, mem_len)
    (x2,) = _mm([mo], [p["w_mem_o"].astype(BF16)], [0], [(tn, F32)], _ep_residual, tm=tm, tn=tn,
                tile_ops=[(x1, 0)], name="mem_o")

    hf = _rmsnorm(x2, p["norm_ffn"], BF16, "norm_ffn")
    down_casts = []
    if _cast_rows_per_step(d_ff, d_ffp, (t // tm) * (d_ffp // tnf)):
        down_casts = [(p["w_ffn_down"], d_ffp, d)]
    act, *down_w = _mm([hf], [w_g, w_u], [0, 0], [(tnf, BF16)], _ep_swiglu, tm=tm, tn=tnf,
                       casts=down_casts, name="ffn_up")
    w_d = down_w[0] if down_w else jnp.pad(p["w_ffn_down"].astype(BF16), ((0, d_ffp - d_ff), (0, 0)))
    x3 = _mm_kgrid_residual(act, w_d, x2, tm=tm, tn=_tile(d, 1024, LANES), tk=d_ffp // 4 if d_ffp >= 2048 else d_ffp,
                            name="ffn_down")
    return x3


def kernel(x, mem, positions, norm_mix, w_in, conv_w, conv_b, lru_wa_f, lru_ba_f, lru_wx_f, lru_bx_f, lru_lam_f,
           lru_wa_b, lru_ba_b, lru_wx_b, lru_bx_b, lru_lam_b, q_a_norm, w_uq, kv_a_norm, w_ukv, w_proj_lru,
           w_proj_mla, w_out, norm_mem_x, norm_mem_kv, w_mem_q, w_mem_kv, w_mem_o, norm_ffn, w_ffn_gate,
           w_ffn_up, w_ffn_down, norm_final):
    bsz, seq, d = x.shape
    stacked = dict(norm_mix=norm_mix, w_in=w_in, conv_w=conv_w, conv_b=conv_b, lru_wa_f=lru_wa_f,
                   lru_ba_f=lru_ba_f, lru_wx_f=lru_wx_f, lru_bx_f=lru_bx_f, lru_lam_f=lru_lam_f,
                   lru_wa_b=lru_wa_b, lru_ba_b=lru_ba_b, lru_wx_b=lru_wx_b, lru_bx_b=lru_bx_b,
                   lru_lam_b=lru_lam_b, q_a_norm=q_a_norm, w_uq=w_uq, kv_a_norm=kv_a_norm, w_ukv=w_ukv,
                   w_proj_lru=w_proj_lru, w_proj_mla=w_proj_mla, w_out=w_out, norm_mem_x=norm_mem_x,
                   norm_mem_kv=norm_mem_kv, w_mem_q=w_mem_q, w_mem_kv=w_mem_kv, w_mem_o=w_mem_o,
                   norm_ffn=norm_ffn, w_ffn_gate=w_ffn_gate, w_ffn_up=w_ffn_up, w_ffn_down=w_ffn_down)
    tables = _rope_tables(positions)
    xf = x.reshape(bsz * seq, d)
    memf = mem.reshape(-1, d)
    for layer in range(norm_mix.shape[0]):
        xf = _layer(xf, memf, tables, bsz, seq, {k: v[layer] for k, v in stacked.items()})
    out = _rmsnorm(xf, norm_final, x.dtype, "norm_final")
    return out.reshape(bsz, seq, d)
```

```python
import functools

import jax
import jax.numpy as jnp
from jax import lax
from jax.experimental import pallas as pl
from jax.experimental.pallas import tpu as pltpu

F32 = jnp.float32
BF16 = jnp.bfloat16

EPS = 1e-6
CONV_WIDTH = 4
LRU_C = 8.0
MLA_HEADS = 32
QK_NOPE = 128
QK_ROPE = 64
V_HEAD = 128
ROPE_THETA = 10000.0
MEM_HEADS = 4
LOG2_E = 1.4426950408889634

LANES = 128
SUBLANES = 8
VMEM_LIMIT_BYTES = 56 * 1024 * 1024

QK_PAD = 2 * LANES
SCAN_PITCH_PAD = 4
LRU_HEADS_PER_STEP = 4


def _tile(n, pref, align):
    if n <= pref:
        return n
    t = (pref // align) * align
    while t > align and n % t:
        t -= align
    assert n % t == 0, (n, pref, align)
    return t


def _params(*sem, flags=None):
    return pltpu.CompilerParams(dimension_semantics=sem, vmem_limit_bytes=VMEM_LIMIT_BYTES, flags=flags)


def _rms(x, g):
    inv = lax.rsqrt(jnp.mean(x * x, axis=-1, keepdims=True) + EPS)
    return x * inv * g


def _rmsnorm_kernel(x_ref, g_ref, o_ref):
    o_ref[...] = _rms(x_ref[...].astype(F32), g_ref[...]).astype(o_ref.dtype)


def _rmsnorm(x, g, out_dtype, name):
    m, d = x.shape
    tm = _tile(m, 256, SUBLANES)
    return pl.pallas_call(
        _rmsnorm_kernel,
        out_shape=jax.ShapeDtypeStruct((m, d), out_dtype),
        grid=(m // tm,),
        in_specs=[pl.BlockSpec((tm, d), lambda i: (i, 0)), pl.BlockSpec((1, d), lambda i: (0, 0))],
        out_specs=pl.BlockSpec((tm, d), lambda i: (i, 0)),
        compiler_params=_params("parallel"),
        name=name,
    )(x, g.reshape(1, d).astype(F32))


def _mm_kernel(*refs, n_a, pairs, n_tile, n_row, n_col, n_out, casts, epilogue):
    a_refs = refs[:n_a]
    w_refs = refs[n_a:n_a + len(pairs)]
    p = n_a + len(pairs)
    tile_refs = refs[p:p + n_tile]
    row_refs = refs[p + n_tile:p + n_tile + n_row]
    col_refs = refs[p + n_tile + n_row:p + n_tile + n_row + n_col]
    p += n_tile + n_row + n_col
    cast_in = refs[p:p + len(casts)]
    o_refs = refs[p + len(casts):p + len(casts) + n_out]
    cast_out = refs[p + len(casts) + n_out:]
    accs = [jnp.dot(a_refs[ai][...], w_ref[...], preferred_element_type=F32)
            for ai, w_ref in zip(pairs, w_refs)]
    epilogue(accs, tile_refs, row_refs, col_refs, o_refs)
    step = pl.program_id(0) * pl.num_programs(1) + pl.program_id(1)
    for src, dst, valid_blocks in zip(cast_in, cast_out, casts):
        cols = src.shape[1]
        dst[:, :cols] = jnp.where(step < valid_blocks, src[...], 0.0).astype(dst.dtype)
        if dst.shape[1] > cols:
            dst[:, cols:] = jnp.zeros((dst.shape[0], dst.shape[1] - cols), dst.dtype)


def _cast_rows_per_step(src_rows, out_rows, steps):
    rps = out_rows // steps
    ok = rps * steps == out_rows and rps % (2 * SUBLANES) == 0 and src_rows % rps == 0
    return rps if ok else None


def _mm(a_ops, w_ops, pairs, outs, epilogue, *, tm, tn, tile_ops=(), row_ops=(), col_ops=(), casts=(), name):
    m = a_ops[0].shape[0]
    n = w_ops[0].shape[1]
    assert m % tm == 0 and n % tn == 0, (m, n, tm, tn)
    nj = n // tn
    cast_specs_in, cast_specs_out, cast_shapes, cast_valid = [], [], [], []
    for src, out_rows, out_cols in casts:
        rps = _cast_rows_per_step(src.shape[0], out_rows, (m // tm) * nj)
        valid = src.shape[0] // rps
        cast_specs_in.append(pl.BlockSpec(
            (rps, src.shape[1]), functools.partial(lambda v, i, j: (jnp.minimum(i * nj + j, v - 1), 0), valid)))
        cast_specs_out.append(pl.BlockSpec((rps, out_cols), lambda i, j: (i * nj + j, 0)))
        cast_shapes.append(jax.ShapeDtypeStruct((out_rows, out_cols), BF16))
        cast_valid.append(valid)
    in_specs = [pl.BlockSpec((tm, a.shape[1]), lambda i, j: (i, 0)) for a in a_ops]
    in_specs += [pl.BlockSpec((w.shape[0], tn), lambda i, j: (0, j)) for w in w_ops]
    in_specs += [pl.BlockSpec((tm, tn), functools.partial(lambda off, i, j: (i, j + off), off))
                 for _, off in tile_ops]
    in_specs += [pl.BlockSpec((tm, r.shape[1]), lambda i, j: (i, 0)) for r in row_ops]
    in_specs += [pl.BlockSpec((1, tn), lambda i, j: (0, j)) for _ in col_ops]
    in_specs += cast_specs_in
    out_shape = [jax.ShapeDtypeStruct((m, (n // tn) * w), dt) for w, dt in outs] + cast_shapes
    out_specs = [pl.BlockSpec((tm, w), lambda i, j: (i, j)) for w, _ in outs] + cast_specs_out
    kern = functools.partial(_mm_kernel, n_a=len(a_ops), pairs=tuple(pairs), n_tile=len(tile_ops),
                             n_row=len(row_ops), n_col=len(col_ops), n_out=len(outs),
                             casts=tuple(cast_valid), epilogue=epilogue)
    res = pl.pallas_call(
        kern,
        out_shape=out_shape,
        grid=(m // tm, n // tn),
        in_specs=in_specs,
        out_specs=out_specs,
        compiler_params=_params("parallel", "arbitrary"),
        name=name,
    )(*a_ops, *w_ops, *[t for t, _ in tile_ops], *row_ops, *col_ops, *[c[0] for c in casts])
    return res


def _ep_plain(accs, tiles, rows, cols, outs):
    outs[0][...] = accs[0].astype(outs[0].dtype)


def _ep_gelu(accs, tiles, rows, cols, outs):
    outs[0][...] = jax.nn.gelu(accs[0]).astype(outs[0].dtype)


def _ep_sigmoid(accs, tiles, rows, cols, outs):
    outs[0][...] = jax.nn.sigmoid(accs[0]).astype(outs[0].dtype)


def _ep_rms(accs, tiles, rows, cols, outs):
    outs[0][...] = _rms(accs[0], cols[0][...]).astype(outs[0].dtype)


def _ep_scale(scale, accs, tiles, rows, cols, outs):
    outs[0][...] = (accs[0] * scale).astype(outs[0].dtype)


def _ep_residual(accs, tiles, rows, cols, outs):
    outs[0][...] = (tiles[0][...] + accs[0]).astype(outs[0].dtype)


def _ep_merge(accs, tiles, rows, cols, outs):
    outs[0][...] = (tiles[0][...] * accs[0] + tiles[1][...] * accs[1]).astype(outs[0].dtype)


def _ep_swiglu(accs, tiles, rows, cols, outs):
    outs[0][...] = (jax.nn.silu(accs[0]) * accs[1]).astype(outs[0].dtype)


def _rope(x, cos, sin_lo, sin_hi):
    half = QK_ROPE // 2
    return x * cos + pltpu.roll(x, half, 1) * sin_hi + pltpu.roll(x, LANES - half, 1) * sin_lo


def _ep_ckv(kv_lora, accs, tiles, rows, cols, outs):
    acc = accs[0]
    outs[0][...] = _rms(acc[:, :kv_lora], cols[0][:, :kv_lora]).astype(outs[0].dtype)
    kr = _rope(acc[:, kv_lora:kv_lora + LANES], rows[0][...], rows[1][...], rows[2][...])
    outs[1][...] = kr.astype(outs[1].dtype)


def _ep_q(scale, accs, tiles, rows, cols, outs):
    acc = accs[0]
    cos, sin_lo, sin_hi = rows[0][...], rows[1][...], rows[2][...]
    for h in range(acc.shape[1] // QK_PAD):
        c0 = h * QK_PAD
        outs[0][:, c0:c0 + QK_NOPE] = (acc[:, c0:c0 + QK_NOPE] * scale).astype(outs[0].dtype)
        rp = _rope(acc[:, c0 + QK_NOPE:c0 + QK_PAD], cos, sin_lo, sin_hi) * scale
        outs[0][:, c0 + QK_NOPE:c0 + QK_PAD] = rp.astype(outs[0].dtype)


def _ep_k(accs, tiles, rows, cols, outs):
    acc = accs[0]
    kr = rows[0][...]
    for h in range(acc.shape[1] // QK_NOPE):
        outs[0][:, h * QK_PAD:h * QK_PAD + QK_NOPE] = acc[:, h * QK_NOPE:(h + 1) * QK_NOPE].astype(outs[0].dtype)
        outs[0][:, h * QK_PAD + QK_NOPE:(h + 1) * QK_PAD] = kr


def _ep_v(accs, tiles, rows, cols, outs):
    acc = accs[0]
    for h in range(acc.shape[1] // V_HEAD):
        outs[0][:, 2 * h * V_HEAD:(2 * h + 1) * V_HEAD] = acc[:, h * V_HEAD:(h + 1) * V_HEAD].astype(outs[0].dtype)
        outs[0][:, (2 * h + 1) * V_HEAD:(2 * h + 2) * V_HEAD] = jnp.ones((acc.shape[0], V_HEAD), outs[0].dtype)


def _mm_kgrid_kernel(a_ref, w_ref, r_ref, o_ref, acc_ref):
    k = pl.program_id(2)

    @pl.when(k == 0)
    def _():
        acc_ref[...] = jnp.zeros_like(acc_ref)

    acc_ref[...] += jnp.dot(a_ref[...], w_ref[...], preferred_element_type=F32)

    @pl.when(k == pl.num_programs(2) - 1)
    def _():
        o_ref[...] = (r_ref[...] + acc_ref[...]).astype(o_ref.dtype)


def _mm_kgrid_residual(a, w, res, *, tm, tn, tk, name):
    m, kdim = a.shape
    n = w.shape[1]
    assert m % tm == 0 and n % tn == 0 and kdim % tk == 0
    return pl.pallas_call(
        _mm_kgrid_kernel,
        out_shape=jax.ShapeDtypeStruct((m, n), res.dtype),
        grid=(m // tm, n // tn, kdim // tk),
        in_specs=[pl.BlockSpec((tm, tk), lambda i, j, k: (i, k)),
                  pl.BlockSpec((tk, tn), lambda i, j, k: (k, j)),
                  pl.BlockSpec((tm, tn), lambda i, j, k: (i, j))],
        out_specs=pl.BlockSpec((tm, tn), lambda i, j, k: (i, j)),
        scratch_shapes=[pltpu.VMEM((tm, tn), F32)],
        compiler_params=_params("parallel", "parallel", "arbitrary"),
        name=name,
    )(a, w, res)


def _rope_table_kernel(pos_ref, f_ref, cos_ref, slo_ref, shi_ref):
    ang = pos_ref[...] * f_ref[...]
    lane = lax.broadcasted_iota(jnp.int32, ang.shape, 1)
    half = QK_ROPE // 2
    c = jnp.cos(ang)
    s = jnp.sin(ang)
    cos_ref[...] = jnp.where(lane < QK_ROPE, c, 0.0)
    slo_ref[...] = jnp.where(lane < half, -s, 0.0)
    shi_ref[...] = jnp.where((lane >= half) & (lane < QK_ROPE), s, 0.0)


def _rope_tables(positions):
    t = positions.size
    inv_freq = ROPE_THETA ** (-jnp.arange(0, QK_ROPE, 2, dtype=F32) / QK_ROPE)
    f = jnp.concatenate([inv_freq, inv_freq, jnp.zeros((LANES - QK_ROPE,), F32)]).reshape(1, LANES)
    pos = positions.astype(F32).reshape(t, 1)
    tm = _tile(t, 1024, SUBLANES)
    shp = jax.ShapeDtypeStruct((t, LANES), F32)
    return pl.pallas_call(
        _rope_table_kernel,
        out_shape=[shp, shp, shp],
        grid=(t // tm,),
        in_specs=[pl.BlockSpec((tm, 1), lambda i: (i, 0)), pl.BlockSpec((1, LANES), lambda i: (0, 0))],
        out_specs=[pl.BlockSpec((tm, LANES), lambda i: (i, 0))] * 3,
        compiler_params=_params("parallel"),
        name="rope_tables",
    )(pos, f)


def _sigmoid(x):
    return 0.5 * jnp.tanh(0.5 * x) + 0.5


def _log_sigmoid(x):
    return jnp.minimum(x, 0.0) - jnp.log1p(jnp.exp(-jnp.abs(x)))


def _conv_centred(x, prev, nxt, cw_ref, cb_ref):
    ts, bw = x.shape
    row8 = lax.broadcasted_iota(jnp.int32, (SUBLANES, bw), 0)

    def shifted(d):
        if d == 0:
            return x
        if d < 0:
            r = pltpu.roll(x, -d, 0)
            head = jnp.where(row8 < -d, pltpu.roll(prev, -d, 0), r[:SUBLANES])
            return jnp.concatenate([head, r[SUBLANES:]], axis=0)
        r = pltpu.roll(x, ts - d, 0)
        tail = jnp.where(row8 >= SUBLANES - d, pltpu.roll(nxt, SUBLANES - d, 0), r[ts - SUBLANES:])
        return jnp.concatenate([r[:ts - SUBLANES], tail], axis=0)

    left = CONV_WIDTH // 2
    xc = cb_ref[...]
    for k in range(CONV_WIDTH):
        xc = xc + shifted(k - left) * cw_ref[k:k + 1, :]
    return xc


def _lru_kernel(*refs, reverse, ts, nc, seq):
    if reverse:
        xc_ref, wa_ref, ba_ref, wx_ref, bx_ref, lam_ref, hf_ref, gg_ref, o_ref = refs[:9]
        refs = refs[9:]
    else:
        (xa_ref, prev_ref, next_ref, cw_ref, cb_ref, wa_ref, ba_ref, wx_ref, bx_ref, lam_ref,
         o_ref, xc_ref) = refs[:12]
        refs = refs[12:]
    a_s, b_s, h_s, p_s, carry = refs
    bw = xc_ref.shape[-1]
    nl = bw // LANES
    seg = ts // SUBLANES
    pitch = seg + SCAN_PITCH_PAD

    c = pl.program_id(2)
    cc = (nc - 1 - c) if reverse else c

    @pl.when(c == 0)
    def _():
        carry[...] = jnp.zeros_like(carry)

    if reverse:
        xc = xc_ref[...]
    else:
        prev = jnp.where(cc > 0, prev_ref[...], 0.0)
        nxt = jnp.where(cc < nc - 1, next_ref[...], 0.0)
        xc = _conv_centred(xa_ref[...], prev, nxt, cw_ref, cb_ref)
        xc_ref[...] = xc

    xb = xc.astype(BF16)
    hw = wa_ref.shape[-1]

    def gate(w_ref, b_ref):
        z = [jnp.dot(xb[:, g * hw:(g + 1) * hw], w_ref[g], preferred_element_type=F32)
             for g in range(w_ref.shape[0])]
        return _sigmoid(jnp.concatenate(z, axis=1) + b_ref[...])

    r = gate(wa_ref, ba_ref)
    i = gate(wx_ref, bx_ref)
    log_a = r * (LRU_C * _log_sigmoid(lam_ref[...]))
    a = jnp.exp(log_a)
    m2 = -jnp.tanh(log_a) * (1.0 + a * a)
    mult = jnp.where(m2 > 0.0, m2 * lax.rsqrt(m2), 0.0)
    row = lax.broadcasted_iota(jnp.int32, (ts, 1), 0) + cc * ts
    mult = jnp.where(row == (seq - 1 if reverse else 0), 1.0, mult)
    b = mult * (i * xc)

    for j in range(SUBLANES):
        for l in range(nl):
            a_s[l, j * pitch:j * pitch + seg, :] = a[j * seg:(j + 1) * seg, l * LANES:(l + 1) * LANES]
            b_s[l, j * pitch:j * pitch + seg, :] = b[j * seg:(j + 1) * seg, l * LANES:(l + 1) * LANES]
    hloc = [jnp.zeros((SUBLANES, LANES), F32) for _ in range(nl)]
    prod = [jnp.ones((SUBLANES, LANES), F32) for _ in range(nl)]
    for t in (range(seg - 1, -1, -1) if reverse else range(seg)):
        for l in range(nl):
            at = a_s[l, pl.ds(t, SUBLANES, stride=pitch), :]
            bt = b_s[l, pl.ds(t, SUBLANES, stride=pitch), :]
            hloc[l] = at * hloc[l] + bt
            prod[l] = at * prod[l]
            h_s[l, pl.ds(t, SUBLANES, stride=pitch), :] = hloc[l]
            p_s[l, pl.ds(t, SUBLANES, stride=pitch), :] = prod[l]
    cj = [carry[:, l * LANES:(l + 1) * LANES] for l in range(nl)]
    for j in (range(SUBLANES - 1, -1, -1) if reverse else range(SUBLANES)):
        rows = slice(j * seg, (j + 1) * seg)
        for l in range(nl):
            lanes = slice(l * LANES, (l + 1) * LANES)
            h = h_s[l, j * pitch:j * pitch + seg, :] + p_s[l, j * pitch:j * pitch + seg, :] * cj[l]
            if reverse:
                h = gg_ref[rows, lanes] * (hf_ref[rows, lanes] + h)
            o_ref[rows, lanes] = h.astype(o_ref.dtype)
            cj[l] = hloc[l][j:j + 1, :] + prod[l][j:j + 1, :] * cj[l]
    for l in range(nl):
        carry[:, l * LANES:(l + 1) * LANES] = cj[l]


def _lru(x, conv, wa, ba, wx, bx, lam, *, reverse, hf=None, gg=None, out_dtype, name):
    bsz, seq, w = x.shape
    heads, hw, _ = wa.shape
    hps = max(g for g in range(1, LRU_HEADS_PER_STEP + 1) if heads % g == 0)
    bw = hps * hw
    ts = _tile(seq, 512, SUBLANES * SUBLANES)
    nc = seq // ts
    nhalo = ts // SUBLANES

    def cidx(c):
        return (nc - 1 - c) if reverse else c

    blk = lambda b, h, c: (b, cidx(c), h)
    chunk = pl.BlockSpec((None, ts, bw), blk)
    row = pl.BlockSpec((1, bw), lambda b, h, c: (0, h))
    gate_w = pl.BlockSpec((hps, hw, hw), lambda b, h, c: (h, 0, 0))
    gate_specs = [gate_w, row, gate_w, row, row]
    gate_args = [wa.astype(BF16), ba.reshape(1, w).astype(F32), wx.astype(BF16), bx.reshape(1, w).astype(F32),
                 lam.reshape(1, w).astype(F32)]
    if reverse:
        in_specs = [chunk] + gate_specs + [chunk, chunk]
        args = [x] + gate_args + [hf, gg]
        out_shape = jax.ShapeDtypeStruct((bsz, seq, w), out_dtype)
        out_specs = chunk
    else:
        conv_w, conv_b = conv
        in_specs = [
            chunk,
            pl.BlockSpec((None, SUBLANES, bw), lambda b, h, c: (b, jnp.maximum(c * nhalo - 1, 0), h)),
            pl.BlockSpec((None, SUBLANES, bw),
                         lambda b, h, c: (b, jnp.minimum((c + 1) * nhalo, seq // SUBLANES - 1), h)),
            pl.BlockSpec((CONV_WIDTH, bw), lambda b, h, c: (0, h)),
            row,
        ] + gate_specs
        args = [x, x, x, conv_w.astype(F32), conv_b.reshape(1, w).astype(F32)] + gate_args
        out_shape = [jax.ShapeDtypeStruct((bsz, seq, w), out_dtype), jax.ShapeDtypeStruct((bsz, seq, w), F32)]
        out_specs = [chunk, chunk]
    nl = bw // LANES
    rows = SUBLANES * (ts // SUBLANES + SCAN_PITCH_PAD)
    scan_buf = pltpu.VMEM((nl, rows, LANES), F32)
    kern = functools.partial(_lru_kernel, reverse=reverse, ts=ts, nc=nc, seq=seq)
    return pl.pallas_call(
        kern,
        out_shape=out_shape,
        grid=(bsz, heads // hps, nc),
        in_specs=in_specs,
        out_specs=out_specs,
        scratch_shapes=[scan_buf, scan_buf, scan_buf, scan_buf, pltpu.VMEM((1, bw), F32)],
        compiler_params=_params("parallel", "parallel", "arbitrary"),
        name=name,
    )(*args)


def _mla_kernel(q_ref, k_ref, v_ref, o_ref, m_s, acc_s, s0_s, s1_s, *, tq, tk):
    seq = k_ref.shape[0]
    nq, nk = seq // tq, seq // tk
    reps = tk // LANES
    s_bufs = (s0_s, s1_s)

    def scores(qi, kj):
        qoff = pl.multiple_of(qi * tq, tq)
        return lax.dot_general(q_ref[pl.ds(qoff, tq), :], k_ref[kj * tk:(kj + 1) * tk, :],
                               (((1,), (1,)), ((), ())), preferred_element_type=F32)

    def consume(s_ref, kj):
        s = s_ref[...]
        m_blk = jnp.max(s, axis=-1, keepdims=True)
        m_new = jnp.broadcast_to(m_blk, (tq, LANES)) if kj == 0 else jnp.maximum(m_s[...], m_blk)
        p = jnp.exp2(s - jnp.concatenate([m_new] * reps, axis=1))
        pv = jnp.dot(p.astype(BF16), v_ref[kj * tk:(kj + 1) * tk, :], preferred_element_type=F32)
        if kj == 0:
            acc_s[...] = pv
        else:
            alpha = jnp.exp2(m_s[...] - m_new)
            acc_s[...] = jnp.concatenate([alpha, alpha], axis=1) * acc_s[...] + pv
        m_s[...] = m_new

    s0_s[...] = scores(0, 0)

    def body(qi, _):
        for kj in range(nk):
            nxt = s_bufs[(kj + 1) % 2]
            if kj + 1 < nk:
                nxt[...] = scores(qi, kj + 1)
            else:
                nxt[...] = scores(jnp.minimum(qi + 1, nq - 1), 0)
            consume(s_bufs[kj % 2], kj)
        qoff = pl.multiple_of(qi * tq, tq)
        o_ref[pl.ds(qoff, tq), :] = (acc_s[:, :V_HEAD] / acc_s[:, V_HEAD:]).astype(o_ref.dtype)
        return 0

    lax.fori_loop(0, nq, body, 0)


def _mla_attention(q, k, v, bsz, seq):
    heads = q.shape[1] // QK_PAD
    tq = _tile(seq, 512, SUBLANES)
    tk = _tile(seq // 2, 1024, LANES)
    assert (seq // tk) % 2 == 0
    kern = functools.partial(_mla_kernel, tq=tq, tk=tk)
    return pl.pallas_call(
        kern,
        out_shape=jax.ShapeDtypeStruct((bsz * seq, heads * V_HEAD), BF16),
        grid=(bsz, heads),
        in_specs=[pl.BlockSpec((seq, QK_PAD), lambda b, h: (b, h)),
                  pl.BlockSpec((seq, QK_PAD), lambda b, h: (b, h)),
                  pl.BlockSpec((seq, 2 * V_HEAD), lambda b, h: (b, h))],
        out_specs=pl.BlockSpec((seq, V_HEAD), lambda b, h: (b, h)),
        scratch_shapes=[pltpu.VMEM((tq, LANES), F32), pltpu.VMEM((tq, 2 * V_HEAD), F32),
                        pltpu.VMEM((tq, tk), F32), pltpu.VMEM((tq, tk), F32)],
        compiler_params=_params("parallel", "arbitrary"),
        name="mla_attention",
    )(q, k, v)


def _mem_attn_kernel(q_ref, kv_ref, o_ref, *, heads, hd):
    for h in range(heads):
        q = q_ref[:, h * hd:(h + 1) * hd]
        k = kv_ref[:, 2 * h * hd:(2 * h + 1) * hd]
        v = kv_ref[:, (2 * h + 1) * hd:(2 * h + 2) * hd]
        s = lax.dot_general(q, k, (((1,), (1,)), ((), ())), preferred_element_type=F32)
        p = jnp.exp(s - jnp.max(s, axis=-1, keepdims=True))
        l = jnp.sum(p, axis=-1, keepdims=True)
        o = jnp.dot(p.astype(BF16), v, preferred_element_type=F32) / l
        o_ref[:, h * hd:(h + 1) * hd] = o.astype(o_ref.dtype)


def _mem_attention(q, kv, bsz, seq, mem_len):
    width = q.shape[1]
    hd = width // MEM_HEADS
    tq = _tile(seq, 512, SUBLANES)
    nq = seq // tq
    kern = functools.partial(_mem_attn_kernel, heads=MEM_HEADS, hd=hd)
    return pl.pallas_call(
        kern,
        out_shape=jax.ShapeDtypeStruct((bsz * seq, width), BF16),
        grid=(bsz, nq),
        in_specs=[pl.BlockSpec((tq, width), lambda b, i: (b * nq + i, 0)),
                  pl.BlockSpec((mem_len, 2 * width), lambda b, i: (b, 0))],
        out_specs=pl.BlockSpec((tq, width), lambda b, i: (b * nq + i, 0)),
        compiler_params=_params("parallel", "arbitrary"),
        name="mem_attention",
    )(q, kv)


def _layer(x, mem, tables, bsz, seq, p):
    t, d = x.shape
    w_in = p["w_in"]
    lru_w = p["conv_w"].shape[1]
    q_lora = p["q_a_norm"].shape[0]
    kv_lora = p["kv_a_norm"].shape[0]
    cos, sin_lo, sin_hi = tables
    tm = _tile(t, 1024, SUBLANES)

    o_xa, o_ga, o_cq, o_ckv, o_kr = 0, lru_w, 2 * lru_w, 2 * lru_w + q_lora, 2 * lru_w + q_lora + kv_lora
    o_gate = o_kr + QK_ROPE
    w_xa = w_in[:, o_xa:o_ga].astype(BF16)
    w_ga = w_in[:, o_ga:o_cq].astype(BF16)
    w_cq = w_in[:, o_cq:o_ckv].astype(BF16)
    w_ckv = jnp.pad(w_in[:, o_ckv:o_gate], ((0, 0), (0, LANES - QK_ROPE))).astype(BF16)
    w_gates = w_in[:, o_gate:].astype(BF16)

    h1 = _rmsnorm(x, p["norm_mix"], BF16, "norm_mix")
    tn = _tile(lru_w, 512, LANES)
    (xa,) = _mm([h1], [w_xa], [0], [(tn, F32)], _ep_plain, tm=tm, tn=tn, name="proj_xa")
    (gg,) = _mm([h1], [w_ga], [0], [(tn, F32)], _ep_gelu, tm=tm, tn=tn, name="proj_ga")
    tn = _tile(2 * d, 512, LANES)
    proj_names = ("w_proj_lru", "w_proj_mla", "w_out")
    proj_casts = []
    if all(_cast_rows_per_step(p[nm].shape[0], p[nm].shape[0], (t // tm) * (2 * d // tn)) for nm in proj_names):
        proj_casts = [(p[nm], p[nm].shape[0], p[nm].shape[1]) for nm in proj_names]
    gates, *proj_w = _mm([h1], [w_gates], [0], [(tn, F32)], _ep_sigmoid, tm=tm, tn=tn, casts=proj_casts,
                         name="proj_gates")
    if not proj_w:
        proj_w = [p[nm].astype(BF16) for nm in proj_names]
    w_proj_lru, w_proj_mla, w_out = proj_w
    (cqn,) = _mm([h1], [w_cq], [0], [(q_lora, BF16)], _ep_rms, tm=tm, tn=q_lora,
                 col_ops=[p["q_a_norm"].reshape(1, q_lora).astype(F32)], name="proj_cq")
    g_kv = jnp.pad(p["kv_a_norm"].astype(F32), (0, LANES)).reshape(1, kv_lora + LANES)
    ckvn, k_rope = _mm([h1], [w_ckv], [0], [(kv_lora, BF16), (LANES, BF16)],
                       functools.partial(_ep_ckv, kv_lora), tm=tm, tn=kv_lora + LANES,
                       row_ops=[cos, sin_lo, sin_hi], col_ops=[g_kv], name="proj_ckv")

    xa3 = xa.reshape(bsz, seq, lru_w)
    h_f, xc = _lru(xa3, (p["conv_w"], p["conv_b"]), p["lru_wa_f"], p["lru_ba_f"], p["lru_wx_f"], p["lru_bx_f"],
                   p["lru_lam_f"], reverse=False, out_dtype=F32, name="lru_fwd")
    y_a = _lru(xc, None, p["lru_wa_b"], p["lru_ba_b"], p["lru_wx_b"], p["lru_bx_b"],
               p["lru_lam_b"], reverse=True, hf=h_f, gg=gg.reshape(bsz, seq, lru_w), out_dtype=BF16,
               name="lru_bwd").reshape(t, lru_w)

    heads = MLA_HEADS
    scale = float(QK_NOPE + QK_ROPE) ** -0.5 * LOG2_E
    w_uq = jnp.pad(p["w_uq"].reshape(q_lora, heads, QK_NOPE + QK_ROPE),
                   ((0, 0), (0, 0), (0, QK_PAD - QK_NOPE - QK_ROPE))).reshape(q_lora, heads * QK_PAD).astype(BF16)
    w_ukv = p["w_ukv"].reshape(kv_lora, heads, QK_NOPE + V_HEAD)
    w_uk = w_ukv[:, :, :QK_NOPE].reshape(kv_lora, heads * QK_NOPE).astype(BF16)
    w_uv = w_ukv[:, :, QK_NOPE:].reshape(kv_lora, heads * V_HEAD).astype(BF16)
    hpt = min(4, heads)
    (q,) = _mm([cqn], [w_uq], [0], [(hpt * QK_PAD, BF16)], functools.partial(_ep_q, scale), tm=tm,
               tn=hpt * QK_PAD, row_ops=[cos, sin_lo, sin_hi], name="proj_q")
    (k,) = _mm([ckvn], [w_uk], [0], [(hpt * QK_PAD, BF16)], _ep_k, tm=tm, tn=hpt * QK_NOPE,
               row_ops=[k_rope], name="proj_k")
    (v,) = _mm([ckvn], [w_uv], [0], [(2 * hpt * V_HEAD, BF16)], _ep_v, tm=tm, tn=hpt * V_HEAD, name="proj_v")
    y_b = _mla_attention(q, k, v, bsz, seq)

    d_ff = p["w_ffn_gate"].shape[1]
    tnf = 512 if d_ff >= 512 else LANES
    d_ffp = -(-d_ff // (2 * tnf)) * (2 * tnf)

    tn = _tile(d, 512, LANES)
    tm2 = _tile(t, 512, SUBLANES)
    up_casts = []
    if _cast_rows_per_step(d, d, (t // tm2) * (d // tn)):
        up_casts = [(p["w_ffn_gate"], d, d_ffp), (p["w_ffn_up"], d, d_ffp)]
    merged, *up_w = _mm([y_a, y_b], [w_proj_lru, w_proj_mla], [0, 1],
                        [(tn, BF16)], _ep_merge, tm=tm2, tn=tn, tile_ops=[(gates, 0), (gates, d // tn)],
                        casts=up_casts, name="proj_merge")
    if not up_w:
        up_w = [jnp.pad(p[nm].astype(BF16), ((0, 0), (0, d_ffp - d_ff))) for nm in ("w_ffn_gate", "w_ffn_up")]
    w_g, w_u = up_w
    (x1,) = _mm([merged], [w_out], [0], [(tn, F32)], _ep_residual, tm=tm, tn=tn,
                tile_ops=[(x, 0)], name="proj_out")

    mem_w = p["w_mem_q"].shape[1]
    mem_len = mem.shape[0] // bsz
    hq = _rmsnorm(x1, p["norm_mem_x"], BF16, "norm_mem_x")
    hm = _rmsnorm(mem, p["norm_mem_kv"], BF16, "norm_mem_kv")
    tnm = _tile(mem_w, 512, LANES)
    (mq,) = _mm([hq], [p["w_mem_q"].astype(BF16)], [0], [(tnm, BF16)],
                functools.partial(_ep_scale, float(mem_w // MEM_HEADS) ** -0.5), tm=tm, tn=tnm, name="mem_q")
    (mkv,) = _mm([hm], [p["w_mem_kv"].astype(BF16)], [0], [(tnm, BF16)], _ep_plain,
                 tm=_tile(mem.shape[0], 512, SUBLANES), tn=tnm, name="mem_kv")
    mo = _mem_attention(mq, mkv, bsz, seq, mem_len)
    (x2,) = _mm([mo], [p["w_mem_o"].astype(BF16)], [0], [(tn, F32)], _ep_residual, tm=tm, tn=tn,
                tile_ops=[(x1, 0)], name="mem_o")

    hf = _rmsnorm(x2, p["norm_ffn"], BF16, "norm_ffn")
    down_casts = []
    if _cast_rows_per_step(d_ff, d_ffp, (t // tm) * (d_ffp // tnf)):
        down_casts = [(p["w_ffn_down"], d_ffp, d)]
    act, *down_w = _mm([hf], [w_g, w_u], [0, 0], [(tnf, BF16)], _ep_swiglu, tm=tm, tn=tnf,
                       casts=down_casts, name="ffn_up")
    w_d = down_w[0] if down_w else jnp.pad(p["w_ffn_down"].astype(BF16), ((0, d_ffp - d_ff), (0, 0)))
    x3 = _mm_kgrid_residual(act, w_d, x2, tm=tm, tn=_tile(d, 1024, LANES), tk=d_ffp // 4 if d_ffp >= 2048 else d_ffp,
                            name="ffn_down")
    return x3


def kernel(x, mem, positions, norm_mix, w_in, conv_w, conv_b, lru_wa_f, lru_ba_f, lru_wx_f, lru_bx_f, lru_lam_f,
           lru_wa_b, lru_ba_b, lru_wx_b, lru_bx_b, lru_lam_b, q_a_norm, w_uq, kv_a_norm, w_ukv, w_proj_lru,
           w_proj_mla, w_out, norm_mem_x, norm_mem_kv, w_mem_q, w_mem_kv, w_mem_o, norm_ffn, w_ffn_gate,
           w_ffn_up, w_ffn_down, norm_final):
    bsz, seq, d = x.shape
    stacked = dict(norm_mix=norm_mix, w_in=w_in, conv_w=conv_w, conv_b=conv_b, lru_wa_f=lru_wa_f,
                   lru_ba_f=lru_ba_f, lru_wx_f=lru_wx_f, lru_bx_f=lru_bx_f, lru_lam_f=lru_lam_f,
                   lru_wa_b=lru_wa_b, lru_ba_b=lru_ba_b, lru_wx_b=lru_wx_b, lru_bx_b=lru_bx_b,
                   lru_lam_b=lru_lam_b, q_a_norm=q_a_norm, w_uq=w_uq, kv_a_norm=kv_a_norm, w_ukv=w_ukv,
                   w_proj_lru=w_proj_lru, w_proj_mla=w_proj_mla, w_out=w_out, norm_mem_x=norm_mem_x,
                   norm_mem_kv=norm_mem_kv, w_mem_q=w_mem_q, w_mem_kv=w_mem_kv, w_mem_o=w_mem_o,
                   norm_ffn=norm_ffn, w_ffn_gate=w_ffn_gate, w_ffn_up=w_ffn_up, w_ffn_down=w_ffn_down)
    tables = _rope_tables(positions)
    xf = x.reshape(bsz * seq, d)
    memf = mem.reshape(-1, d)
    for layer in range(norm_mix.shape[0]):
        xf = _layer(xf, memf, tables, bsz, seq, {k: v[layer] for k, v in stacked.items()})
    out = _rmsnorm(xf, norm_final, x.dtype, "norm_final")
    return out.reshape(bsz, seq, d)
```
